```python
import jax
import jax.numpy as jnp
from jax import lax
import numpy as np

D_MODEL = 4096
BATCH = 4
SEQ = 2048
DEPTH = 2
DEC_BATCH = 8
DEC_SEQ = 1
PAST_LEN = 16384
PAGE_SIZE = 128

N_A = DEPTH // 2
N_B = DEPTH - N_A
N_DENSE = (DEPTH + 1) // 2
N_MOE = DEPTH // 2
MEM_LEN = 256
MEM_HEADS = 4
MEM_HEAD_DIM = D_MODEL // 16
MEM_WIDTH = MEM_HEADS * MEM_HEAD_DIM
MAIN_WIDTH = D_MODEL - MEM_WIDTH
CONV_DIM = MAIN_WIDTH
CONV_W = 3
HEAD_DIM = 128
N_HEADS = MAIN_WIDTH // HEAD_DIM
N_KV_HEADS = 8
GROUP = N_HEADS // N_KV_HEADS
KV_WIDTH = N_KV_HEADS * HEAD_DIM
D_FF = 7 * D_MODEL // 2
N_EXPERTS = 8
TOP_K = 2
EXPERT_FF = 7 * D_MODEL // 2
Q_BLOCK = 128
FORGET_BIAS = 2.0
EPS = 1e-6

kernel_name = 'yoco_shortconv_fox_mem_moe_step'

F32 = jnp.float32


def rmsnorm(x, g):
    xf = x.astype(F32)
    y = xf * lax.rsqrt(jnp.mean(xf * xf, axis=-1, keepdims=True) + EPS)
    return (y * g.astype(F32)).astype(x.dtype)


def swiglu(x, w_gu, w_d):
    g, u = jnp.split(x @ w_gu, 2, axis=-1)
    return (jax.nn.silu(g) * u) @ w_d


def moe_ffn(x, w_router, b_router, we_gu, we_down):
    xt = x.reshape(-1, x.shape[-1])
    logits = (xt @ w_router).astype(F32) + b_router.astype(F32)
    top_v, top_i = lax.top_k(logits, TOP_K)
    gates = jax.nn.softmax(top_v, axis=-1)
    combine = jnp.sum(jax.nn.one_hot(top_i, N_EXPERTS, dtype=F32) * gates[..., None], axis=1)
    out = jnp.zeros_like(xt)
    for e in range(N_EXPERTS):
        out = out + combine[:, e:e + 1].astype(xt.dtype) * swiglu(xt, we_gu[e], we_down[e])
    return out.reshape(x.shape)


def short_conv_mixer(xn, conv_prev, w_in, conv_w):
    z = xn @ w_in
    b_gate, c_gate, hv, q_mem = jnp.split(z, [CONV_DIM, 2 * CONV_DIM, 3 * CONV_DIM], axis=-1)
    u = c_gate * hv
    u_full = jnp.concatenate([conv_prev.astype(u.dtype), u], axis=1)
    t = u.shape[1]
    y = conv_w[0] * u_full[:, 0:t]
    for i in range(1, CONV_W):
        y = y + conv_w[i] * u_full[:, i:i + t]
    return b_gate * y, q_mem, u_full[:, -(CONV_W - 1):]


def fox_attend(q, k, v, cq, ck, q_pos, k_pos):
    b, tq, h, dh = q.shape
    tk = k.shape[1]
    blk = min(Q_BLOCK, tq)
    nb = -(-tq // blk)
    pad = nb * blk - tq
    if pad:
        q = jnp.pad(q, ((0, 0), (0, pad), (0, 0), (0, 0)))
        cq = jnp.pad(cq, ((0, 0), (0, pad), (0, 0)), mode='edge')
        q_pos = jnp.pad(q_pos, (0, pad), mode='edge')
    qb = q.reshape(b, nb, blk, N_KV_HEADS, GROUP, dh).transpose(1, 0, 2, 3, 4, 5)
    cqb = cq.reshape(b, nb, blk, N_KV_HEADS, GROUP).transpose(1, 0, 3, 4, 2)
    pb = q_pos.reshape(nb, blk)
    ckg = ck.reshape(b, tk, N_KV_HEADS, GROUP).transpose(0, 2, 3, 1)
    scale = HEAD_DIM ** -0.5

    def block(args):
        qi, ci, pi = args
        s = jnp.einsum('bqkgd,bskd->bkgqs', qi, k).astype(F32) * scale
        s = s + (ci[..., :, None] - ckg[..., None, :])
        s = jnp.where(k_pos[None, :] <= pi[:, None], s, -jnp.inf)
        p = jax.nn.softmax(s, axis=-1).astype(v.dtype)
        return jnp.einsum('bkgqs,bskd->bqkgd', p, v)

    o = lax.map(block, (qb, cqb, pb))
    return o.transpose(1, 0, 2, 3, 4, 5).reshape(b, nb * blk, h, dh)[:, :tq]


def fox_mixer(xn, kv, w_in, q_norm):
    k, v, cq, ck, q_pos, k_pos = kv
    b, t, _ = xn.shape
    z = xn @ w_in
    q = rmsnorm(z[..., :MAIN_WIDTH].reshape(b, t, N_HEADS, HEAD_DIM), q_norm)
    o = fox_attend(q, k, v, cq, ck, q_pos, k_pos)
    return o.reshape(b, t, MAIN_WIDTH), z[..., MAIN_WIDTH:]


def mem_attend(q_mem, q_norm, mem_k, mem_v):
    b, t, _ = q_mem.shape
    q = rmsnorm(q_mem.reshape(b, t, MEM_HEADS, MEM_HEAD_DIM), q_norm)
    s = jnp.einsum('bthd,bmhd->bhtm', q, mem_k).astype(F32) * (MEM_HEAD_DIM ** -0.5)
    p = jax.nn.softmax(s, axis=-1).astype(mem_v.dtype)
    return jnp.einsum('bhtm,bmhd->bthd', p, mem_v).reshape(b, t, MEM_WIDTH)


def mem_kv_proj(mem, w, k_norm):
    b, m, _ = mem.shape
    z = mem @ w
    k = rmsnorm(z[..., :MEM_WIDTH].reshape(b, m, MEM_HEADS, MEM_HEAD_DIM), k_norm)
    v = z[..., MEM_WIDTH:].reshape(b, m, MEM_HEADS, MEM_HEAD_DIM)
    return k, v


def shared_kv(h, kv_norm, w_kv, b_f, k_norm):
    b, t, _ = h.shape
    z = rmsnorm(h, kv_norm) @ w_kv
    k = rmsnorm(z[..., :KV_WIDTH].reshape(b, t, N_KV_HEADS, HEAD_DIM), k_norm)
    v = z[..., KV_WIDTH:2 * KV_WIDTH].reshape(b, t, N_KV_HEADS, HEAD_DIM)
    logf = jax.nn.log_sigmoid(z[..., 2 * KV_WIDTH:].astype(F32) + b_f.astype(F32))
    return k, v, logf


def trunk(x, conv_prev, mem_k, mem_v, make_kv, w):
    h = x
    conv_new = []
    kv, kv_rows = None, None
    for i in range(DEPTH):
        if i == N_A:
            kv, kv_rows = make_kv(h)
        xn = rmsnorm(h, w['norm_mix'][i])
        if i < N_A:
            main, q_mem, cs = short_conv_mixer(xn, conv_prev[i], w['w_in_a'][i], w['conv_w'][i])
            conv_new.append(cs)
        else:
            main, q_mem = fox_mixer(xn, kv, w['w_in_b'][i - N_A], w['q_norm_b'][i - N_A])
        mo = mem_attend(q_mem, w['mem_q_norm'][i], mem_k[i], mem_v[i])
        h = h + jnp.concatenate([main, mo], axis=-1) @ w['w_out'][i]
        hn = rmsnorm(h, w['norm_ffn'][i])
        j = i // 2
        if i % 2 == 0:
            h = h + swiglu(hn, w['w_gu'][j], w['w_down'][j])
        else:
            h = h + moe_ffn(hn, w['w_router'][j], w['b_router'][j], w['we_gu'][j], w['we_down'][j])
    return h, jnp.stack(conv_new), kv_rows


def setup_inputs(seed: int = 0) -> dict:
    key = jax.random.key(seed)
    ks = iter(jax.random.split(key, 40))

    def nrm(shape, scale=1.0):
        return jax.random.normal(next(ks), shape, F32) * scale

    n_pages = PAST_LEN // PAGE_SIZE
    n_used = DEC_BATCH * n_pages
    n_pool = (5 * n_used + 3) // 4
    page_table = jax.random.permutation(next(ks), n_pool)[:n_used].reshape(DEC_BATCH, n_pages).astype(jnp.int32)
    d = D_MODEL
    return {
        'x_prompt': nrm((BATCH, SEQ, d)),
        'x_sample': nrm((DEC_BATCH, DEC_SEQ, d)),
        'mem_prompt': nrm((BATCH, MEM_LEN, d)),
        'cache_k': nrm((n_pool, PAGE_SIZE, N_KV_HEADS, HEAD_DIM)),
        'cache_v': nrm((n_pool, PAGE_SIZE, N_KV_HEADS, HEAD_DIM)),
        'cache_logf': jax.nn.log_sigmoid(FORGET_BIAS + nrm((n_pool, PAGE_SIZE, N_HEADS))),
        'state_conv': nrm((N_A, DEC_BATCH, CONV_W - 1, CONV_DIM)),
        'cache_mem_k': nrm((DEPTH, DEC_BATCH, MEM_LEN, MEM_HEADS, MEM_HEAD_DIM)),
        'cache_mem_v': nrm((DEPTH, DEC_BATCH, MEM_LEN, MEM_HEADS, MEM_HEAD_DIM)),
        'page_table': page_table,
        'norm_mix': 1.0 + nrm((DEPTH, d), 0.02),
        'norm_ffn': 1.0 + nrm((DEPTH, d), 0.02),
        'w_in_a': nrm((N_A, d, 3 * CONV_DIM + MEM_WIDTH), d ** -0.5),
        'conv_w': nrm((N_A, CONV_W, CONV_DIM), CONV_W ** -0.5),
        'w_in_b': nrm((N_B, d, MAIN_WIDTH + MEM_WIDTH), d ** -0.5),
        'q_norm_b': 1.0 + nrm((N_B, HEAD_DIM), 0.02),
        'w_out': nrm((DEPTH, MAIN_WIDTH + MEM_WIDTH, d), (MAIN_WIDTH + MEM_WIDTH) ** -0.5),
        'w_mem_kv': nrm((DEPTH, d, 2 * MEM_WIDTH), d ** -0.5),
        'mem_q_norm': 1.0 + nrm((DEPTH, MEM_HEAD_DIM), 0.02),
        'mem_k_norm': 1.0 + nrm((DEPTH, MEM_HEAD_DIM), 0.02),
        'kv_norm': 1.0 + nrm((d,), 0.02),
        'w_kv': nrm((d, 2 * KV_WIDTH + N_HEADS), d ** -0.5),
        'b_f': FORGET_BIAS + nrm((N_HEADS,), 0.1),
        'k_norm': 1.0 + nrm((HEAD_DIM,), 0.02),
        'w_gu': nrm((N_DENSE, d, 2 * D_FF), d ** -0.5),
        'w_down': nrm((N_DENSE, D_FF, d), D_FF ** -0.5),
        'w_router': nrm((N_MOE, d, N_EXPERTS), d ** -0.5),
        'b_router': nrm((N_MOE, N_EXPERTS), 0.01),
        'we_gu': nrm((N_MOE, N_EXPERTS, d, 2 * EXPERT_FF), d ** -0.5),
        'we_down': nrm((N_MOE, N_EXPERTS, EXPERT_FF, d), EXPERT_FF ** -0.5),
    }


def reference(x_prompt, x_sample, mem_prompt, cache_k, cache_v, cache_logf, state_conv,
              cache_mem_k, cache_mem_v, page_table, norm_mix, norm_ffn, w_in_a, conv_w,
              w_in_b, q_norm_b, w_out, w_mem_kv, mem_q_norm, mem_k_norm, kv_norm, w_kv,
              b_f, k_norm, w_gu, w_down, w_router, b_router, we_gu, we_down):
    w = {'norm_mix': norm_mix, 'norm_ffn': norm_ffn, 'w_in_a': w_in_a, 'conv_w': conv_w,
         'w_in_b': w_in_b, 'q_norm_b': q_norm_b, 'w_out': w_out, 'mem_q_norm': mem_q_norm,
         'w_gu': w_gu, 'w_down': w_down, 'w_router': w_router, 'b_router': b_router,
         'we_gu': we_gu, 'we_down': we_down}
    n_pages = PAST_LEN // PAGE_SIZE
    past = n_pages * PAGE_SIZE

    mk_list, mv_list = [], []
    for i in range(DEPTH):
        mk, mv = mem_kv_proj(mem_prompt, w_mem_kv[i], mem_k_norm[i])
        mk_list.append(mk)
        mv_list.append(mv)
    mem_k_prompt = jnp.stack(mk_list)
    mem_v_prompt = jnp.stack(mv_list)

    def make_kv_prompt(h):
        k, v, logf = shared_kv(h, kv_norm, w_kv, b_f, k_norm)
        c = lax.cumsum(logf, axis=1)
        pos = jnp.arange(h.shape[1], dtype=jnp.int32)
        return (k, v, c, c, pos, pos), (k, v, logf)

    conv_zero = jnp.zeros((N_A, x_prompt.shape[0], CONV_W - 1, CONV_DIM), x_prompt.dtype)
    y_prompt, conv_prompt, rows_p = trunk(x_prompt, conv_zero, mem_k_prompt, mem_v_prompt, make_kv_prompt, w)
    k_prompt, v_prompt, logf_prompt = rows_p

    def make_kv_sample(h):
        k_new, v_new, logf_new = shared_kv(h, kv_norm, w_kv, b_f, k_norm)
        db, t = h.shape[0], h.shape[1]
        k_past = cache_k[page_table].reshape(db, past, N_KV_HEADS, HEAD_DIM).astype(k_new.dtype)
        v_past = cache_v[page_table].reshape(db, past, N_KV_HEADS, HEAD_DIM).astype(v_new.dtype)
        lf_past = cache_logf[page_table].reshape(db, past, N_HEADS).astype(F32)
        k_all = jnp.concatenate([k_past, k_new], axis=1)
        v_all = jnp.concatenate([v_past, v_new], axis=1)
        c = lax.cumsum(jnp.concatenate([lf_past, logf_new], axis=1), axis=1)
        q_pos = past + jnp.arange(t, dtype=jnp.int32)
        k_pos = jnp.arange(past + t, dtype=jnp.int32)
        return (k_all, v_all, c[:, past:], c, q_pos, k_pos), (k_new, v_new, logf_new)

    y_sample, conv_sample, rows_s = trunk(x_sample, state_conv, cache_mem_k, cache_mem_v, make_kv_sample, w)
    k_sample, v_sample, logf_sample = rows_s

    return (y_prompt, y_sample, k_prompt, v_prompt, logf_prompt, conv_prompt, mem_k_prompt, mem_v_prompt,
            k_sample, v_sample, logf_sample, conv_sample)
```

```python
import functools

import jax
import jax.numpy as jnp
from jax import lax
from jax.experimental import pallas as pl
from jax.experimental.pallas import tpu as pltpu

F32 = jnp.float32
BF16 = jnp.bfloat16
EPS = 1e-6
NEG_INF = float("-inf")

V7X_VMEM_BYTES = 64 * 1024 * 1024
VMEM_LIMIT = V7X_VMEM_BYTES - 8 * 1024 * 1024
LANE = 128
MOE_CHUNK = 2560
MOE_SUB = 256


def _cparams(*sem):
    return pltpu.CompilerParams(dimension_semantics=sem, vmem_limit_bytes=VMEM_LIMIT)


def _tile(n, pref):
    if n <= pref:
        return n
    t = pref
    while n % t:
        t -= LANE
    assert t > 0
    return t


def _dot(a, b):
    return jnp.dot(a, b, preferred_element_type=F32)


def _dot_nt(a, b):
    return lax.dot_general(a, b, (((1,), (1,)), ((), ())), preferred_element_type=F32)


def _rms(x, g):
    return x * lax.rsqrt(jnp.mean(x * x, axis=-1, keepdims=True) + EPS) * g


def _rmsnorm_body(x_ref, g_ref, *o_refs):
    y = _rms(x_ref[...].astype(F32), g_ref[...])
    for o in o_refs:
        o[...] = y.astype(o.dtype)


def _rmsnorm(x, g, dtypes):
    m, d = x.shape
    tr = _tile(m, 256)
    outs = pl.pallas_call(
        _rmsnorm_body,
        grid=(m // tr,),
        in_specs=[pl.BlockSpec((tr, d), lambda i: (i, 0)), pl.BlockSpec((1, d), lambda i: (0, 0))],
        out_specs=[pl.BlockSpec((tr, d), lambda i: (i, 0)) for _ in dtypes],
        out_shape=[jax.ShapeDtypeStruct((m, d), dt) for dt in dtypes],
        compiler_params=_cparams("arbitrary"),
        name="rmsnorm",
    )(x, g.reshape(1, d))
    return outs


def _mm_body(x_ref, w_ref, o_ref):
    o_ref[...] = _dot(x_ref[...].astype(BF16), w_ref[...].astype(BF16)).astype(o_ref.dtype)


def _mm(x, w3, layer, c0, n, out_dtype, tm=1024, tn=512):
    m, k = x.shape
    tm, tn = _tile(m, tm), _tile(n, tn)
    assert c0 % tn == 0
    return pl.pallas_call(
        _mm_body,
        grid=(m // tm, n // tn),
        in_specs=[pl.BlockSpec((tm, k), lambda i, j: (i, 0)),
                  pl.BlockSpec((None, k, tn), lambda i, j: (layer, 0, c0 // tn + j))],
        out_specs=pl.BlockSpec((tm, tn), lambda i, j: (i, j)),
        out_shape=jax.ShapeDtypeStruct((m, n), out_dtype),
        compiler_params=_cparams("arbitrary", "arbitrary"),
        name="mm",
    )(x, w3)


def _mm_headnorm_body(x_ref, w_ref, g_ref, o_ref, *, hd):
    z = _dot(x_ref[...].astype(BF16), w_ref[...].astype(BF16))
    for h in range(z.shape[1] // hd):
        o_ref[:, h * hd:(h + 1) * hd] = _rms(z[:, h * hd:(h + 1) * hd], g_ref[...]).astype(o_ref.dtype)


def _mm_headnorm(x, w3, layer, c0, n, gain, hd, out_dtype, tm=1024, tn=512):
    m, k = x.shape
    tm, tn = _tile(m, tm), _tile(n, tn)
    assert c0 % tn == 0 and tn % hd == 0
    return pl.pallas_call(
        functools.partial(_mm_headnorm_body, hd=hd),
        grid=(m // tm, n // tn),
        in_specs=[pl.BlockSpec((tm, k), lambda i, j: (i, 0)),
                  pl.BlockSpec((None, k, tn), lambda i, j: (layer, 0, c0 // tn + j)),
                  pl.BlockSpec((1, hd), lambda i, j: (0, 0))],
        out_specs=pl.BlockSpec((tm, tn), lambda i, j: (i, j)),
        out_shape=jax.ShapeDtypeStruct((m, n), out_dtype),
        compiler_params=_cparams("arbitrary", "arbitrary"),
        name="mm_headnorm",
    )(x, w3, gain.reshape(1, hd))


def _mm_out_body(x1_ref, x2_ref, w_ref, r_ref, o_ref):
    k1 = x1_ref.shape[1]
    w = w_ref[...].astype(BF16)
    acc = _dot(x1_ref[...].astype(BF16), w[:k1]) + _dot(x2_ref[...].astype(BF16), w[k1:])
    o_ref[...] = r_ref[...] + acc


def _mm_out(x1, x2, w3, layer, resid, tm=1024, tn=512):
    m, k1 = x1.shape
    k2 = x2.shape[1]
    n = w3.shape[2]
    tm, tn = _tile(m, tm), _tile(n, tn)
    return pl.pallas_call(
        _mm_out_body,
        grid=(m // tm, n // tn),
        in_specs=[pl.BlockSpec((tm, k1), lambda i, j: (i, 0)),
                  pl.BlockSpec((tm, k2), lambda i, j: (i, 0)),
                  pl.BlockSpec((None, k1 + k2, tn), lambda i, j: (layer, 0, j)),
                  pl.BlockSpec((tm, tn), lambda i, j: (i, j))],
        out_specs=pl.BlockSpec((tm, tn), lambda i, j: (i, j)),
        out_shape=jax.ShapeDtypeStruct((m, n), F32),
        compiler_params=_cparams("arbitrary", "arbitrary"),
        name="mm_out",
    )(x1, x2, w3, resid)


def _silu_mul(g, u):
    return g * jax.nn.sigmoid(g) * u


def _mm_glu_body(x_ref, wg_ref, wu_ref, o_ref):
    x = x_ref[...].astype(BF16)
    g = _dot(x, wg_ref[...].astype(BF16))
    u = _dot(x, wu_ref[...].astype(BF16))
    o_ref[...] = _silu_mul(g, u).astype(o_ref.dtype)


def _mm_glu(x, w3, layer, out_dtype, tm=1024, tn=256):
    m, k = x.shape
    f = w3.shape[2] // 2
    tm, tn = _tile(m, tm), _tile(f, tn)
    return pl.pallas_call(
        _mm_glu_body,
        grid=(m // tm, f // tn),
        in_specs=[pl.BlockSpec((tm, k), lambda i, j: (i, 0)),
                  pl.BlockSpec((None, k, tn), lambda i, j: (layer, 0, j)),
                  pl.BlockSpec((None, k, tn), lambda i, j: (layer, 0, f // tn + j))],
        out_specs=pl.BlockSpec((tm, tn), lambda i, j: (i, j)),
        out_shape=jax.ShapeDtypeStruct((m, f), out_dtype),
        compiler_params=_cparams("arbitrary", "arbitrary"),
        name="mm_glu",
    )(x, w3, w3)


def _mm_down_body(x_ref, w_ref, r_ref, o_ref):
    kk = pl.program_id(2)
    part = _dot(x_ref[...].astype(BF16), w_ref[...].astype(BF16))

    @pl.when(kk == 0)
    def _():
        o_ref[...] = part

    @pl.when(kk > 0)
    def _():
        o_ref[...] += part

    @pl.when(kk == pl.num_programs(2) - 1)
    def _():
        o_ref[...] = r_ref[...] + o_ref[...]


def _mm_down(x, w3, layer, resid, tm=2048, tn=512, tk=2048):
    m, k = x.shape
    n = w3.shape[2]
    tm, tn, tk = _tile(m, tm), _tile(n, tn), _tile(k, tk)
    return pl.pallas_call(
        _mm_down_body,
        grid=(m // tm, n // tn, k // tk),
        in_specs=[pl.BlockSpec((tm, tk), lambda i, j, kk: (i, kk)),
                  pl.BlockSpec((None, tk, tn), lambda i, j, kk: (layer, kk, j)),
                  pl.BlockSpec((tm, tn), lambda i, j, kk: (i, j))],
        out_specs=pl.BlockSpec((tm, tn), lambda i, j, kk: (i, j)),
        out_shape=jax.ShapeDtypeStruct((m, n), F32),
        compiler_params=_cparams("arbitrary", "arbitrary", "arbitrary"),
        name="mm_down",
    )(x, w3, resid)


def _conv_prompt_body(x_ref, wb_ref, wc_ref, wh_ref, cw_ref, y_ref, tail_ref, carry_ref, *, tiles_per_seq):
    i, j = pl.program_id(0), pl.program_id(1)
    x = x_ref[...]
    b = _dot(x, wb_ref[...].astype(BF16))
    c = _dot(x, wc_ref[...].astype(BF16))
    hv = _dot(x, wh_ref[...].astype(BF16))
    u = c * hv
    tm = u.shape[0]
    @pl.when(i % tiles_per_seq == 0)
    def _():
        carry_ref[j] = jnp.zeros(carry_ref.shape[1:], F32)

    prev = carry_ref[j]
    row = lax.broadcasted_iota(jnp.int32, u.shape, 0)
    u1 = jnp.where(row == 0, prev[1:2], pltpu.roll(u, 1, 0))
    u2 = jnp.where(row == 0, prev[0:1], jnp.where(row == 1, prev[1:2], pltpu.roll(u, 2, 0)))
    cw = cw_ref[...]
    y = cw[0:1] * u2 + cw[1:2] * u1 + cw[2:3] * u
    y_ref[...] = (b * y).astype(y_ref.dtype)
    carry_ref[j] = u[tm - 2:tm]
    tail_ref[...] = u[tm - 2:tm]


def _conv_prompt(xn, w_in_a, conv_w, layer, batch, seq, tm=1024, tn=256):
    m, d = xn.shape
    cd = conv_w.shape[2]
    tm, tn = _tile(seq, tm), _tile(cd, tn)
    nj = cd // tn
    tps = seq // tm
    y, tail = pl.pallas_call(
        functools.partial(_conv_prompt_body, tiles_per_seq=tps),
        grid=(m // tm, nj),
        in_specs=[pl.BlockSpec((tm, d), lambda i, j: (i, 0)),
                  pl.BlockSpec((None, d, tn), lambda i, j: (layer, 0, j)),
                  pl.BlockSpec((None, d, tn), lambda i, j: (layer, 0, nj + j)),
                  pl.BlockSpec((None, d, tn), lambda i, j: (layer, 0, 2 * nj + j)),
                  pl.BlockSpec((None, 3, tn), lambda i, j: (layer, 0, j))],
        out_specs=[pl.BlockSpec((tm, tn), lambda i, j: (i, j)),
                   pl.BlockSpec((None, 2, tn), lambda i, j: (i, 0, j))],
        out_shape=[jax.ShapeDtypeStruct((m, cd), BF16), jax.ShapeDtypeStruct((m // tm, 2, cd), F32)],
        scratch_shapes=[pltpu.VMEM((nj, 2, tn), F32)],
        compiler_params=_cparams("arbitrary", "arbitrary"),
        name="conv_prompt",
    )(xn, w_in_a, w_in_a, w_in_a, conv_w)
    return y, tail[tps - 1::tps]


def _conv_step_body(z_ref, s_ref, cw_ref, y_ref, ns_ref, *, cd):
    b, c, hv = z_ref[:, 0:cd], z_ref[:, cd:2 * cd], z_ref[:, 2 * cd:3 * cd]
    u = c * hv
    s0, s1 = s_ref[:, 0, :], s_ref[:, 1, :]
    cw = cw_ref[...]
    y = cw[0:1] * s0 + cw[1:2] * s1 + cw[2:3] * u
    y_ref[...] = b * y
    ns_ref[:, 0, :] = s1
    ns_ref[:, 1, :] = u


def _conv_step(z, state, cw):
    db, _, cd = state.shape
    return pl.pallas_call(
        functools.partial(_conv_step_body, cd=cd),
        out_shape=[jax.ShapeDtypeStruct((db, cd), F32), jax.ShapeDtypeStruct((db, 2, cd), F32)],
        compiler_params=pltpu.CompilerParams(vmem_limit_bytes=VMEM_LIMIT),
        name="conv_step",
    )(z, state, cw)


def _mem_attend_body(q_ref, g_ref, k_ref, v_ref, o_ref, *, heads):
    hd = q_ref.shape[-1] // heads
    scale = hd ** -0.5
    for h in range(heads):
        sl = slice(h * hd, (h + 1) * hd)
        q = _rms(q_ref[:, sl].astype(F32), g_ref[...]).astype(BF16)
        s = _dot_nt(q, k_ref[:, sl].astype(BF16)) * scale
        s = s - jnp.max(s, axis=-1, keepdims=True)
        p = jnp.exp(s)
        p = p / jnp.sum(p, axis=-1, keepdims=True)
        o_ref[:, sl] = _dot(p.astype(BF16), v_ref[:, sl].astype(BF16)).astype(o_ref.dtype)


def _mem_attend(q3, gain, k3, v3, heads, out_dtype, tq=512):
    b, t, w = q3.shape
    ml = k3.shape[1]
    tq = _tile(t, tq)
    hd = w // heads
    return pl.pallas_call(
        functools.partial(_mem_attend_body, heads=heads),
        grid=(b, t // tq),
        in_specs=[pl.BlockSpec((None, tq, w), lambda i, j: (i, j, 0)),
                  pl.BlockSpec((1, hd), lambda i, j: (0, 0)),
                  pl.BlockSpec((None, ml, w), lambda i, j: (i, 0, 0)),
                  pl.BlockSpec((None, ml, w), lambda i, j: (i, 0, 0))],
        out_specs=pl.BlockSpec((None, tq, w), lambda i, j: (i, j, 0)),
        out_shape=jax.ShapeDtypeStruct((b, t, w), out_dtype),
        compiler_params=_cparams("arbitrary", "arbitrary"),
        name="mem_attend",
    )(q3, gain.reshape(1, hd), k3, v3)


def _log_sigmoid(x):
    return -(jnp.maximum(-x, 0.0) + jnp.log1p(jnp.exp(-jnp.abs(x))))


def _logf_body(x_ref, w_ref, b_ref, o_ref, opad_ref):
    z = _dot(x_ref[...].astype(BF16), w_ref[...].astype(BF16))
    lf = _log_sigmoid(z + b_ref[...])
    opad_ref[...] = lf
    o_ref[...] = lf[:, :o_ref.shape[1]]


def _logf(x, w_pad, b_pad, h, tm=1024):
    m, k = x.shape
    tm = _tile(m, tm)
    return pl.pallas_call(
        _logf_body,
        grid=(m // tm,),
        in_specs=[pl.BlockSpec((tm, k), lambda i: (i, 0)),
                  pl.BlockSpec((k, LANE), lambda i: (0, 0)),
                  pl.BlockSpec((1, LANE), lambda i: (0, 0))],
        out_specs=[pl.BlockSpec((tm, h), lambda i: (i, 0)), pl.BlockSpec((tm, LANE), lambda i: (i, 0))],
        out_shape=[jax.ShapeDtypeStruct((m, h), F32), jax.ShapeDtypeStruct((m, LANE), F32)],
        compiler_params=_cparams("arbitrary"),
        name="logf",
    )(x, w_pad, b_pad)


def _cumsum_body(x_ref, o_ref):
    x = x_ref[...]
    t = x.shape[0]
    row = lax.broadcasted_iota(jnp.int32, x.shape, 0)
    s = 1
    while s < t:
        x = x + jnp.where(row >= s, pltpu.roll(x, s, 0), 0.0)
        s *= 2
    o_ref[...] = x


def _cumsum_time(x3):
    b, t, w = x3.shape
    return pl.pallas_call(
        _cumsum_body,
        grid=(b,),
        in_specs=[pl.BlockSpec((None, t, w), lambda i: (i, 0, 0))],
        out_specs=pl.BlockSpec((None, t, w), lambda i: (i, 0, 0)),
        out_shape=jax.ShapeDtypeStruct((b, t, w), F32),
        compiler_params=_cparams("arbitrary"),
        name="cumsum_time",
    )(x3)


def _fox_prompt_body(q_ref, k_ref, v_ref, cq_ref, ck_ref, o_ref, *, group, hd, tk):
    qi = pl.program_id(2)
    tq = q_ref.shape[0]
    scale = hd ** -0.5
    qpos = qi * tq + lax.broadcasted_iota(jnp.int32, (tq, tk), 0)
    kiota = lax.broadcasted_iota(jnp.int32, (tq, tk), 1)
    for g in range(group):
        q = q_ref[:, g * hd:(g + 1) * hd]
        cq = cq_ref[:, g:g + 1]

        def body(ki, carry, q=q, cq=cq, g=g):
            m, l, acc = carry
            off = pl.multiple_of(ki * tk, tk)
            kc = k_ref[pl.ds(off, tk), :].astype(BF16)
            vc = v_ref[pl.ds(off, tk), :].astype(BF16)
            s = _dot_nt(q, kc) * scale
            s = s + (cq - ck_ref[g:g + 1, pl.ds(off, tk)])
            s = jnp.where(off + kiota <= qpos, s, NEG_INF)
            m_new = jnp.maximum(m, jnp.max(s, axis=-1, keepdims=True))
            alpha = jnp.exp(m - m_new)
            p = jnp.exp(s - m_new)
            l = alpha * l + jnp.sum(p, axis=-1, keepdims=True)
            acc = alpha * acc + _dot(p.astype(BF16), vc)
            return m_new, l, acc

        init = (jnp.full((tq, 1), NEG_INF, F32), jnp.zeros((tq, 1), F32), jnp.zeros((tq, hd), F32))
        _, l, acc = lax.fori_loop(0, qi + 1, body, init)
        o_ref[:, g * hd:(g + 1) * hd] = (acc / l).astype(o_ref.dtype)


def _fox_prompt(qn, k2, v2, c_col, c_row, batch, seq, kvh, group, hd, tq=256):
    m = qn.shape[0]
    tq = _tile(seq, tq)
    nq = seq // tq
    gw = group * hd
    return pl.pallas_call(
        functools.partial(_fox_prompt_body, group=group, hd=hd, tk=tq),
        grid=(batch, kvh, nq),
        in_specs=[pl.BlockSpec((tq, gw), lambda b, h, i: (b * nq + i, h)),
                  pl.BlockSpec((seq, hd), lambda b, h, i: (b, h)),
                  pl.BlockSpec((seq, hd), lambda b, h, i: (b, h)),
                  pl.BlockSpec((None, None, tq, group), lambda b, h, i: (b, h, i, 0)),
                  pl.BlockSpec((None, None, group, seq), lambda b, h, i: (b, h, 0, 0))],
        out_specs=pl.BlockSpec((tq, gw), lambda b, h, i: (b * nq + i, h)),
        out_shape=jax.ShapeDtypeStruct((m, kvh * gw), BF16),
        compiler_params=_cparams("arbitrary", "arbitrary", "arbitrary"),
        name="fox_prompt",
    )(qn, k2, v2, c_col, c_row)


def _fox_decode_body(pt_ref, q_ref, kn_ref, vn_ref, lfn_ref, k_ref, v_ref, lf_ref, o_ref,
                     m_ref, l_ref, acc_ref, d_ref, *, kvh, group, hd):
    p = pl.program_id(1)
    heads = kvh * group
    scale = hd ** -0.5
    head_kv = lax.broadcasted_iota(jnp.int32, (heads, hd), 0) // group
    q = q_ref[...].astype(F32)
    qbd = jnp.concatenate([jnp.where(head_kv == j, q, 0.0) for j in range(kvh)], axis=1)

    @pl.when(p == 0)
    def _():
        m_ref[...] = jnp.sum(qbd * kn_ref[...], axis=1, keepdims=True) * scale
        l_ref[...] = jnp.ones_like(l_ref)
        acc_ref[...] = jnp.broadcast_to(vn_ref[...], acc_ref.shape)
        d_ref[...] = lfn_ref[...]

    lf = lf_ref[...]
    ps = lf.shape[1]
    lane = lax.broadcasted_iota(jnp.int32, lf.shape, 1)
    suf = lf
    s = 1
    while s < ps:
        suf = suf + jnp.where(lane < ps - s, pltpu.roll(suf, ps - s, 1), 0.0)
        s *= 2
    bias = d_ref[...] + (suf - lf)
    sc = _dot_nt(qbd.astype(BF16), k_ref[...].astype(BF16)) * scale + bias
    m_old = m_ref[...]
    m_new = jnp.maximum(m_old, jnp.max(sc, axis=-1, keepdims=True))
    alpha = jnp.exp(m_old - m_new)
    pr = jnp.exp(sc - m_new)
    l_ref[...] = alpha * l_ref[...] + jnp.sum(pr, axis=-1, keepdims=True)
    acc_ref[...] = alpha * acc_ref[...] + _dot(pr.astype(BF16), v_ref[...].astype(BF16))
    m_ref[...] = m_new
    d_ref[...] = d_ref[...] + suf[:, 0:1]

    @pl.when(p == pl.num_programs(1) - 1)
    def _():
        acc = acc_ref[...]
        out = jnp.zeros((heads, hd), F32)
        for j in range(kvh):
            out = out + jnp.where(head_kv == j, acc[:, j * hd:(j + 1) * hd], 0.0)
        o_ref[...] = out / l_ref[...]


def _fox_decode(page_table, q3, k_new, v_new, lf_new, cache_k2, cache_v2, cache_lf_t, kvh, group, hd):
    db, heads, _ = q3.shape
    npg = page_table.shape[1]
    ps = cache_k2.shape[1]
    w = kvh * hd

    def page(b, p, pt):
        return (pt[b, npg - 1 - p], 0, 0)

    grid_spec = pltpu.PrefetchScalarGridSpec(
        num_scalar_prefetch=1,
        grid=(db, npg),
        in_specs=[pl.BlockSpec((None, heads, hd), lambda b, p, pt: (b, 0, 0)),
                  pl.BlockSpec((None, 1, w), lambda b, p, pt: (b, 0, 0)),
                  pl.BlockSpec((None, 1, w), lambda b, p, pt: (b, 0, 0)),
                  pl.BlockSpec((None, heads, 1), lambda b, p, pt: (b, 0, 0)),
                  pl.BlockSpec((None, ps, w), page),
                  pl.BlockSpec((None, ps, w), page),
                  pl.BlockSpec((None, heads, ps), page)],
        out_specs=pl.BlockSpec((None, heads, hd), lambda b, p, pt: (b, 0, 0)),
        scratch_shapes=[pltpu.VMEM((heads, 1), F32), pltpu.VMEM((heads, 1), F32),
                        pltpu.VMEM((heads, w), F32), pltpu.VMEM((heads, 1), F32)],
    )
    return pl.pallas_call(
        functools.partial(_fox_decode_body, kvh=kvh, group=group, hd=hd),
        grid_spec=grid_spec,
        out_shape=jax.ShapeDtypeStruct((db, heads, hd), F32),
        compiler_params=_cparams("arbitrary", "arbitrary"),
        name="fox_decode",
    )(page_table, q3, k_new, v_new, lf_new, cache_k2, cache_v2, cache_lf_t)


def _router_body(x_ref, w_ref, b_ref, idx_ref, gate_ref, *, n_exp):
    logits = jnp.dot(x_ref[...], w_ref[...], preferred_element_type=F32,
                     precision=lax.Precision.HIGHEST) + b_ref[...]
    lane = lax.broadcasted_iota(jnp.int32, logits.shape, 1)
    logits = jnp.where(lane < n_exp, logits, NEG_INF)
    m1 = jnp.max(logits, axis=-1, keepdims=True)
    i1 = jnp.min(jnp.where(logits == m1, lane, LANE), axis=-1, keepdims=True)
    rest = jnp.where(lane == i1, NEG_INF, logits)
    m2 = jnp.max(rest, axis=-1, keepdims=True)
    i2 = jnp.min(jnp.where(rest == m2, lane, LANE), axis=-1, keepdims=True)
    e = jnp.exp(m2 - m1)
    g1 = 1.0 / (1.0 + e)
    g2 = e / (1.0 + e)
    idx_ref[...] = jnp.where(lane == 0, i1, jnp.where(lane == 1, i2, 0))
    gate_ref[...] = jnp.where(lane == 0, g1, jnp.where(lane == 1, g2, 0.0))


def _router(x, w_pad, b_pad, n_exp):
    m, k = x.shape
    tm = _tile(m, 256)
    return pl.pallas_call(
        functools.partial(_router_body, n_exp=n_exp),
        grid=(m // tm,),
        in_specs=[pl.BlockSpec((tm, k), lambda i: (i, 0)),
                  pl.BlockSpec((k, LANE), lambda i: (0, 0)),
                  pl.BlockSpec((1, LANE), lambda i: (0, 0))],
        out_specs=[pl.BlockSpec((tm, LANE), lambda i: (i, 0)), pl.BlockSpec((tm, LANE), lambda i: (i, 0))],
        out_shape=[jax.ShapeDtypeStruct((m, LANE), jnp.int32), jax.ShapeDtypeStruct((m, LANE), F32)],
        compiler_params=_cparams("arbitrary"),
        name="router",
    )(x, w_pad, b_pad)


def _gather_body(tok_ref, used_ref, a_ref, b_ref, o_ref, buf_ref, sem, *, tg):
    i = pl.program_id(0)
    ma = a_ref.shape[0]
    base = i * tg

    def row_copy(src_ref, t, r):
        return pltpu.make_async_copy(src_ref.at[pl.ds(t, 1)], buf_ref.at[pl.ds(r, 1)], sem)

    @pl.when(base < used_ref[0])
    def _():
        def issue(r, c):
            t = tok_ref[base + r]

            @pl.when(t < ma)
            def _():
                row_copy(a_ref, t, r).start()

            @pl.when(t >= ma)
            def _():
                row_copy(b_ref, t - ma, r).start()
            return c

        lax.fori_loop(0, tg, issue, 0)

        def drain(r, c):
            row_copy(a_ref, 0, r).wait()
            return c

        lax.fori_loop(0, tg, drain, 0)
        o_ref[...] = buf_ref[...].astype(o_ref.dtype)

    @pl.when(base >= used_ref[0])
    def _():
        o_ref[...] = jnp.zeros_like(o_ref)


def _gather_rows(row_token, used_rows, src_a, src_b, tg=256):
    p = row_token.shape[0]
    d = src_a.shape[1]
    tg = _tile(p, tg)
    grid_spec = pltpu.PrefetchScalarGridSpec(
        num_scalar_prefetch=2,
        grid=(p // tg,),
        in_specs=[pl.BlockSpec(memory_space=pl.ANY), pl.BlockSpec(memory_space=pl.ANY)],
        out_specs=pl.BlockSpec((tg, d), lambda i, tok, used: (i, 0)),
        scratch_shapes=[pltpu.VMEM((tg, d), F32), pltpu.SemaphoreType.DMA(())],
    )
    return pl.pallas_call(
        functools.partial(_gather_body, tg=tg),
        grid_spec=grid_spec,
        out_shape=jax.ShapeDtypeStruct((p, d), BF16),
        compiler_params=_cparams("arbitrary"),
        name="moe_gather",
    )(row_token, used_rows, src_a, src_b)


def _combine_body(pos_ref, y_ref, h_ref, g_ref, o_ref, buf_ref, sem, *, tt):
    i = pl.program_id(0)
    base = i * tt

    def row_copy(k, src_row, r):
        return pltpu.make_async_copy(y_ref.at[pl.ds(src_row, 1)], buf_ref.at[k, pl.ds(r, 1)], sem)

    def issue(r, c):
        row_copy(0, pos_ref[2 * (base + r)], r).start()
        row_copy(1, pos_ref[2 * (base + r) + 1], r).start()
        return c

    lax.fori_loop(0, tt, issue, 0)

    def drain(r, c):
        row_copy(0, 0, r).wait()
        row_copy(1, 0, r).wait()
        return c

    lax.fori_loop(0, tt, drain, 0)
    g = g_ref[...]
    o_ref[...] = h_ref[...] + (g[:, 0:1] * buf_ref[0] + g[:, 1:2] * buf_ref[1])


def _combine(pos_flat, y_sorted, h, gates, tt=256):
    m, d = h.shape
    tt = _tile(m, tt)
    grid_spec = pltpu.PrefetchScalarGridSpec(
        num_scalar_prefetch=1,
        grid=(m // tt,),
        in_specs=[pl.BlockSpec(memory_space=pl.ANY),
                  pl.BlockSpec((tt, d), lambda i, pos: (i, 0)),
                  pl.BlockSpec((tt, LANE), lambda i, pos: (i, 0))],
        out_specs=pl.BlockSpec((tt, d), lambda i, pos: (i, 0)),
        scratch_shapes=[pltpu.VMEM((2, tt, d), F32), pltpu.SemaphoreType.DMA(())],
    )
    return pl.pallas_call(
        functools.partial(_combine_body, tt=tt),
        grid_spec=grid_spec,
        out_shape=jax.ShapeDtypeStruct((m, d), F32),
        compiler_params=_cparams("arbitrary"),
        name="moe_combine",
    )(pos_flat, y_sorted, h, gates)


def _moe_glu_body(ce_ref, cx_ref, cv_ref, used_ref, x_ref, wg_ref, wu_ref, o_ref, wgs_ref, wus_ref, *, sub):
    c = pl.program_id(0)
    nsub = x_ref.shape[0] // sub

    @pl.when(c < used_ref[0])
    def _():
        wgs_ref[...] = wg_ref[...].astype(BF16)
        wus_ref[...] = wu_ref[...].astype(BF16)
        nvalid = (cv_ref[c] + sub - 1) // sub

        def live(sb, carry):
            rows = pl.ds(pl.multiple_of(sb * sub, sub), sub)
            x = x_ref[rows, :]
            g = _dot(x, wgs_ref[...])
            u = _dot(x, wus_ref[...])
            o_ref[rows, :] = _silu_mul(g, u).astype(o_ref.dtype)
            return carry

        lax.fori_loop(0, nvalid, live, 0)

        def dead(sb, carry):
            rows = pl.ds(pl.multiple_of(sb * sub, sub), sub)
            o_ref[rows, :] = jnp.zeros((sub, o_ref.shape[1]), o_ref.dtype)
            return carry

        lax.fori_loop(nvalid, nsub, dead, 0)

    @pl.when(c >= used_ref[0])
    def _():
        o_ref[...] = jnp.zeros_like(o_ref)


def _moe_glu(sched, x_sorted, we_gu, chunk, sub, tn=256):
    ce, cx, cv, used = sched
    p, d = x_sorted.shape
    f = we_gu.shape[2] // 2
    tn = _tile(f, tn)
    nj = f // tn
    nc = p // chunk

    def jpin(c, j, used):
        return jnp.where(c < used[0], j, nj - 1)

    grid_spec = pltpu.PrefetchScalarGridSpec(
        num_scalar_prefetch=4,
        grid=(nc, nj),
        in_specs=[pl.BlockSpec((chunk, d), lambda c, j, ce, cx, cv, used: (cx[c], 0),
                               pipeline_mode=pl.Buffered(1)),
                  pl.BlockSpec((None, d, tn), lambda c, j, ce, cx, cv, used: (ce[c], 0, jpin(c, j, used))),
                  pl.BlockSpec((None, d, tn), lambda c, j, ce, cx, cv, used: (ce[c], 0, nj + jpin(c, j, used)))],
        out_specs=pl.BlockSpec((chunk, tn), lambda c, j, ce, cx, cv, used: (c, j)),
        scratch_shapes=[pltpu.VMEM((d, tn), BF16), pltpu.VMEM((d, tn), BF16)],
    )
    return pl.pallas_call(
        functools.partial(_moe_glu_body, sub=sub),
        grid_spec=grid_spec,
        out_shape=jax.ShapeDtypeStruct((p, f), BF16),
        compiler_params=_cparams("arbitrary", "arbitrary"),
        name="moe_glu",
    )(ce, cx, cv, used, x_sorted, we_gu, we_gu)


def _moe_down_body(ce_ref, cx_ref, cv_ref, used_ref, x_ref, w_ref, o_ref, ws_ref, *, sub):
    c, kk = pl.program_id(0), pl.program_id(2)
    nsub = x_ref.shape[0] // sub

    @pl.when(c < used_ref[0])
    def _():
        ws_ref[...] = w_ref[...].astype(BF16)
        nvalid = (cv_ref[c] + sub - 1) // sub

        def live(sb, carry):
            rows = pl.ds(pl.multiple_of(sb * sub, sub), sub)
            part = _dot(x_ref[rows, :], ws_ref[...])

            @pl.when(kk == 0)
            def _():
                o_ref[rows, :] = part

            @pl.when(kk > 0)
            def _():
                o_ref[rows, :] += part
            return carry

        lax.fori_loop(0, nvalid, live, 0)

        @pl.when(kk == 0)
        def _():
            def dead(sb, carry):
                rows = pl.ds(pl.multiple_of(sb * sub, sub), sub)
                o_ref[rows, :] = jnp.zeros((sub, o_ref.shape[1]), o_ref.dtype)
                return carry

            lax.fori_loop(nvalid, nsub, dead, 0)

    @pl.when((c >= used_ref[0]) & (kk == 0))
    def _():
        o_ref[...] = jnp.zeros_like(o_ref)


def _moe_down(sched, act_sorted, we_down, chunk, sub, tn=1024, tk=1024):
    ce, cx, cv, used = sched
    p, f = act_sorted.shape
    d = we_down.shape[2]
    tn, tk = _tile(d, tn), _tile(f, tk)
    nj, nk = d // tn, f // tk
    nc = p // chunk

    def pin(c, v, last, used):
        return jnp.where(c < used[0], v, last)

    grid_spec = pltpu.PrefetchScalarGridSpec(
        num_scalar_prefetch=4,
        grid=(nc, nj, nk),
        in_specs=[pl.BlockSpec((chunk, tk), lambda c, j, k, ce, cx, cv, used: (cx[c], pin(c, k, nk - 1, used))),
                  pl.BlockSpec((None, tk, tn), lambda c, j, k, ce, cx, cv, used:
                               (ce[c], pin(c, k, nk - 1, used), pin(c, j, nj - 1, used)))],
        out_specs=pl.BlockSpec((chunk, tn), lambda c, j, k, ce, cx, cv, used: (c, j)),
        scratch_shapes=[pltpu.VMEM((tk, tn), BF16)],
    )
    return pl.pallas_call(
        functools.partial(_moe_down_body, sub=sub),
        grid_spec=grid_spec,
        out_shape=jax.ShapeDtypeStruct((p, d), F32),
        compiler_params=_cparams("arbitrary", "arbitrary", "arbitrary"),
        name="moe_down",
    )(ce, cx, cv, used, act_sorted, we_down)


def _moe(h_p, h_s, hn_p, hn_s, w_router, b_router, we_gu, we_down):
    mp, d = hn_p.shape
    ms = hn_s.shape[0]
    n_exp = we_gu.shape[0]
    mt = mp + ms
    w_pad = jnp.pad(w_router, ((0, 0), (0, LANE - n_exp)))
    b_pad = jnp.pad(b_router.reshape(1, n_exp).astype(F32), ((0, 0), (0, LANE - n_exp)))
    idx_p, gate_p = _router(hn_p, w_pad, b_pad, n_exp)
    idx_s, gate_s = _router(hn_s, w_pad, b_pad, n_exp)

    chunk = min(MOE_CHUNK, -(-2 * mt // MOE_SUB) * MOE_SUB)
    sub = min(MOE_SUB, chunk)
    idx = jnp.concatenate([idx_p[:, :2], idx_s[:, :2]], axis=0)
    sel = (idx[:, :, None] == jnp.arange(n_exp, dtype=jnp.int32)[None, None, :]).any(axis=1).astype(jnp.int32)
    rank = jnp.cumsum(sel, axis=0) - sel
    cnt = jnp.sum(sel, axis=0)
    nch = (cnt + chunk - 1) // chunk
    ch_end = jnp.cumsum(nch)
    ch_off = ch_end - nch
    n_chunks = (2 * mt + n_exp * (chunk - 1)) // chunk
    p_rows = n_chunks * chunk
    pos = jnp.take_along_axis(ch_off[None, :] * chunk + rank, idx, axis=1)
    tok = jnp.repeat(jnp.arange(mt, dtype=jnp.int32), 2)
    row_token = jnp.zeros((p_rows,), jnp.int32).at[pos.reshape(-1)].set(tok)
    used = ch_end[-1].astype(jnp.int32)
    cid = jnp.minimum(jnp.arange(n_chunks, dtype=jnp.int32), used - 1)
    ce = jnp.minimum(jnp.searchsorted(ch_end, cid, side="right"), n_exp - 1).astype(jnp.int32)
    cv = jnp.clip(cnt[ce] - (cid - ch_off[ce]) * chunk, 0, chunk).astype(jnp.int32)
    sched = (ce, cid, cv, used.reshape(1))

    x_sorted = _gather_rows(row_token, (used * chunk).reshape(1), hn_p, hn_s)
    act = _moe_glu(sched, x_sorted, we_gu, chunk, sub)
    y_sorted = _moe_down(sched, act, we_down, chunk, sub)
    pos = pos.astype(jnp.int32)
    out_p = _combine(pos[:mp].reshape(-1), y_sorted, h_p, gate_p)
    out_s = _combine(pos[mp:].reshape(-1), y_sorted, h_s, gate_s)
    return out_p, out_s


def kernel(x_prompt, x_sample, mem_prompt, cache_k, cache_v, cache_logf, state_conv, cache_mem_k, cache_mem_v, page_table, norm_mix, norm_ffn, w_in_a, conv_w, w_in_b, q_norm_b, w_out, w_mem_kv, mem_q_norm, mem_k_norm, kv_norm, w_kv, b_f, k_norm, w_gu, w_down, w_router, b_router, we_gu, we_down):
    bsz, seq, d = x_prompt.shape
    db, dec_seq, _ = x_sample.shape
    assert dec_seq == 1, "the decode path handles one new token per sequence"
    depth, _, ml, mem_heads, mem_hd = cache_mem_k.shape
    assert depth == 2 and w_in_a.shape[0] == 1 and w_in_b.shape[0] == 1
    mem_w = mem_heads * mem_hd
    n_pool, ps, kvh, hd = cache_k.shape
    heads = cache_logf.shape[2]
    group = heads // kvh
    kv_w = kvh * hd
    cd = conv_w.shape[2]
    assert conv_w.shape[1] == 3 and state_conv.shape[2] == 2
    mp = bsz * seq

    mem_bf = mem_prompt.reshape(bsz * ml, d).astype(BF16)
    mk, mv = [], []
    for i in range(depth):
        mk.append(_mm_headnorm(mem_bf, w_mem_kv, i, 0, mem_w, mem_k_norm[i], mem_hd, F32))
        mv.append(_mm(mem_bf, w_mem_kv, i, mem_w, mem_w, F32))
    mem_k_prompt = jnp.stack(mk).reshape(depth, bsz, ml, mem_heads, mem_hd)
    mem_v_prompt = jnp.stack(mv).reshape(depth, bsz, ml, mem_heads, mem_hd)

    w_kv3 = w_kv.reshape(1, d, -1)
    w_lf = jnp.pad(w_kv[:, 2 * kv_w:], ((0, 0), (0, LANE - heads)))
    b_lf = jnp.pad(b_f.reshape(1, heads).astype(F32), ((0, 0), (0, LANE - heads)))
    we_gu3 = we_gu.reshape(we_gu.shape[1:])
    we_down3 = we_down.reshape(we_down.shape[1:])

    def layer0_tail(h0, main, q_mem, mem_k0, mem_v0, nb, act_dtype):
        t = h0.shape[0] // nb
        mo = _mem_attend(q_mem.reshape(nb, t, mem_w), mem_q_norm[0], mem_k0.reshape(nb, ml, mem_w),
                         mem_v0.reshape(nb, ml, mem_w), mem_heads, act_dtype).reshape(nb * t, mem_w)
        h1 = _mm_out(main, mo, w_out, 0, h0)
        (hn,) = _rmsnorm(h1, norm_ffn[0], (act_dtype,))
        act = _mm_glu(hn, w_gu, 0, act_dtype)
        h2 = _mm_down(act, w_down, 0, h1)
        (kvn,) = _rmsnorm(h2, kv_norm, (act_dtype,))
        k = _mm_headnorm(kvn, w_kv3, 0, 0, kv_w, k_norm, hd, F32)
        v = _mm(kvn, w_kv3, 0, kv_w, kv_w, F32)
        logf, logf_pad = _logf(kvn, w_lf, b_lf, heads)
        (xn1,) = _rmsnorm(h2, norm_mix[1], (act_dtype,))
        qn = _mm_headnorm(xn1, w_in_b, 0, 0, heads * hd, q_norm_b[0], hd, act_dtype)
        q_mem1 = _mm(xn1, w_in_b, 0, heads * hd, mem_w, F32)
        return h2, k, v, logf, logf_pad, qn, q_mem1

    def layer1_attn_out(h2, attn, q_mem1, mem_k1, mem_v1, nb, act_dtype):
        t = h2.shape[0] // nb
        mo = _mem_attend(q_mem1.reshape(nb, t, mem_w), mem_q_norm[1], mem_k1.reshape(nb, ml, mem_w),
                         mem_v1.reshape(nb, ml, mem_w), mem_heads, act_dtype).reshape(nb * t, mem_w)
        h3 = _mm_out(attn, mo, w_out, 1, h2)
        (hn3,) = _rmsnorm(h3, norm_ffn[1], (F32,))
        return h3, hn3

    xp = x_prompt.reshape(mp, d)
    (xn,) = _rmsnorm(xp, norm_mix[0], (BF16,))
    main_p, conv_tail = _conv_prompt(xn, w_in_a, conv_w, 0, bsz, seq)
    q_mem_p = _mm(xn, w_in_a, 0, 3 * cd, mem_w, F32)
    h2_p, k_p, v_p, logf_p, logf_pad_p, qn_p, q_mem1_p = layer0_tail(
        xp, main_p, q_mem_p, mk[0], mv[0], bsz, BF16)
    c = _cumsum_time(logf_pad_p.reshape(bsz, seq, LANE))[:, :, :heads].reshape(bsz, seq, kvh, group)
    attn_p = _fox_prompt(qn_p, k_p, v_p, c.transpose(0, 2, 1, 3), c.transpose(0, 2, 3, 1),
                         bsz, seq, kvh, group, hd)
    h3_p, hn3_p = layer1_attn_out(h2_p, attn_p, q_mem1_p, mk[1], mv[1], bsz, BF16)

    xs = x_sample.reshape(db, d)
    (xn_s,) = _rmsnorm(xs, norm_mix[0], (F32,))
    z_s = _mm(xn_s, w_in_a, 0, 0, 3 * cd + mem_w, F32)
    main_s, conv_new_s = _conv_step(z_s, state_conv[0], conv_w[0])
    h2_s, k_s, v_s, logf_s, _, qn_s, q_mem1_s = layer0_tail(
        xs, main_s, z_s[:, 3 * cd:], cache_mem_k[0], cache_mem_v[0], db, F32)
    attn_s = _fox_decode(page_table, qn_s.reshape(db, heads, hd), k_s.reshape(db, 1, kv_w),
                         v_s.reshape(db, 1, kv_w), logf_s.reshape(db, heads, 1),
                         cache_k.reshape(n_pool, ps, kv_w), cache_v.reshape(n_pool, ps, kv_w),
                         cache_logf.transpose(0, 2, 1), kvh, group, hd).reshape(db, heads * hd)
    h3_s, hn3_s = layer1_attn_out(h2_s, attn_s, q_mem1_s, cache_mem_k[1], cache_mem_v[1], db, F32)

    y_p, y_s = _moe(h3_p, h3_s, hn3_p, hn3_s, w_router[0], b_router[0], we_gu3, we_down3)

    return (y_p.reshape(bsz, seq, d), y_s.reshape(db, 1, d),
            k_p.reshape(bsz, seq, kvh, hd), v_p.reshape(bsz, seq, kvh, hd), logf_p.reshape(bsz, seq, heads),
            conv_tail.reshape(1, bsz, 2, cd), mem_k_prompt, mem_v_prompt,
            k_s.reshape(db, 1, kvh, hd), v_s.reshape(db, 1, kvh, hd), logf_s.reshape(db, 1, heads),
            conv_new_s.reshape(1, db, 2, cd))
```

```python
import functools

import jax
import jax.numpy as jnp
from jax import lax
from jax.experimental import pallas as pl
from jax.experimental.pallas import tpu as pltpu

F32 = jnp.float32
BF16 = jnp.bfloat16
EPS = 1e-6
NEG_INF = float("-inf")

V7X_VMEM_BYTES = 64 * 1024 * 1024
VMEM_LIMIT = V7X_VMEM_BYTES - 8 * 1024 * 1024
LANE = 128
MOE_CHUNK = 2560
MOE_BIG = 512
MOE_SMALL = 128


def _cparams(*sem):
    return pltpu.CompilerParams(dimension_semantics=sem, vmem_limit_bytes=VMEM_LIMIT)


def _tile(n, pref):
    if n <= pref:
        return n
    t = pref
    while n % t:
        t -= LANE
    assert t > 0
    return t


def _dot(a, b):
    return jnp.dot(a, b, preferred_element_type=F32)


def _dot_nt(a, b):
    return lax.dot_general(a, b, (((1,), (1,)), ((), ())), preferred_element_type=F32)


def _rms(x, g):
    return x * lax.rsqrt(jnp.mean(x * x, axis=-1, keepdims=True) + EPS) * g


def _rmsnorm_body(x_ref, g_ref, *o_refs):
    y = _rms(x_ref[...].astype(F32), g_ref[...])
    for o in o_refs:
        o[...] = y.astype(o.dtype)


def _rmsnorm(x, g, dtypes):
    m, d = x.shape
    tr = _tile(m, 256)
    outs = pl.pallas_call(
        _rmsnorm_body,
        grid=(m // tr,),
        in_specs=[pl.BlockSpec((tr, d), lambda i: (i, 0)), pl.BlockSpec((1, d), lambda i: (0, 0))],
        out_specs=[pl.BlockSpec((tr, d), lambda i: (i, 0)) for _ in dtypes],
        out_shape=[jax.ShapeDtypeStruct((m, d), dt) for dt in dtypes],
        compiler_params=_cparams("arbitrary"),
        name="rmsnorm",
    )(x, g.reshape(1, d))
    return outs


def _mm_body(x_ref, w_ref, o_ref):
    o_ref[...] = _dot(x_ref[...].astype(BF16), w_ref[...].astype(BF16)).astype(o_ref.dtype)


def _mm(x, w3, layer, c0, n, out_dtype, tm=1024, tn=512):
    m, k = x.shape
    tm, tn = _tile(m, tm), _tile(n, tn)
    assert c0 % tn == 0
    return pl.pallas_call(
        _mm_body,
        grid=(m // tm, n // tn),
        in_specs=[pl.BlockSpec((tm, k), lambda i, j: (i, 0)),
                  pl.BlockSpec((None, k, tn), lambda i, j: (layer, 0, c0 // tn + j))],
        out_specs=pl.BlockSpec((tm, tn), lambda i, j: (i, j)),
        out_shape=jax.ShapeDtypeStruct((m, n), out_dtype),
        compiler_params=_cparams("arbitrary", "arbitrary"),
        name="mm",
    )(x, w3)


def _mm_headnorm_body(x_ref, w_ref, g_ref, o_ref, *, hd):
    z = _dot(x_ref[...].astype(BF16), w_ref[...].astype(BF16))
    for h in range(z.shape[1] // hd):
        o_ref[:, h * hd:(h + 1) * hd] = _rms(z[:, h * hd:(h + 1) * hd], g_ref[...]).astype(o_ref.dtype)


def _mm_headnorm(x, w3, layer, c0, n, gain, hd, out_dtype, tm=1024, tn=512):
    m, k = x.shape
    tm, tn = _tile(m, tm), _tile(n, tn)
    assert c0 % tn == 0 and tn % hd == 0
    return pl.pallas_call(
        functools.partial(_mm_headnorm_body, hd=hd),
        grid=(m // tm, n // tn),
        in_specs=[pl.BlockSpec((tm, k), lambda i, j: (i, 0)),
                  pl.BlockSpec((None, k, tn), lambda i, j: (layer, 0, c0 // tn + j)),
                  pl.BlockSpec((1, hd), lambda i, j: (0, 0))],
        out_specs=pl.BlockSpec((tm, tn), lambda i, j: (i, j)),
        out_shape=jax.ShapeDtypeStruct((m, n), out_dtype),
        compiler_params=_cparams("arbitrary", "arbitrary"),
        name="mm_headnorm",
    )(x, w3, gain.reshape(1, hd))


def _mm_out_body(x1_ref, x2_ref, w_ref, r_ref, o_ref):
    k1 = x1_ref.shape[1]
    w = w_ref[...].astype(BF16)
    acc = _dot(x1_ref[...].astype(BF16), w[:k1]) + _dot(x2_ref[...].astype(BF16), w[k1:])
    o_ref[...] = r_ref[...] + acc


def _mm_out(x1, x2, w3, layer, resid, tm=1024, tn=512):
    m, k1 = x1.shape
    k2 = x2.shape[1]
    n = w3.shape[2]
    tm, tn = _tile(m, tm), _tile(n, tn)
    return pl.pallas_call(
        _mm_out_body,
        grid=(m // tm, n // tn),
        in_specs=[pl.BlockSpec((tm, k1), lambda i, j: (i, 0)),
                  pl.BlockSpec((tm, k2), lambda i, j: (i, 0)),
                  pl.BlockSpec((None, k1 + k2, tn), lambda i, j: (layer, 0, j)),
                  pl.BlockSpec((tm, tn), lambda i, j: (i, j))],
        out_specs=pl.BlockSpec((tm, tn), lambda i, j: (i, j)),
        out_shape=jax.ShapeDtypeStruct((m, n), F32),
        compiler_params=_cparams("arbitrary", "arbitrary"),
        name="mm_out",
    )(x1, x2, w3, resid)


def _silu_mul(g, u):
    return g * jax.nn.sigmoid(g) * u


def _mm_glu_body(x_ref, wg_ref, wu_ref, o_ref):
    x = x_ref[...].astype(BF16)
    g = _dot(x, wg_ref[...].astype(BF16))
    u = _dot(x, wu_ref[...].astype(BF16))
    o_ref[...] = _silu_mul(g, u).astype(o_ref.dtype)


def _mm_glu(x, w3, layer, out_dtype, tm=1024, tn=256):
    m, k = x.shape
    f = w3.shape[2] // 2
    tm, tn = _tile(m, tm), _tile(f, tn)
    return pl.pallas_call(
        _mm_glu_body,
        grid=(m // tm, f // tn),
        in_specs=[pl.BlockSpec((tm, k), lambda i, j: (i, 0)),
                  pl.BlockSpec((None, k, tn), lambda i, j: (layer, 0, j)),
                  pl.BlockSpec((None, k, tn), lambda i, j: (layer, 0, f // tn + j))],
        out_specs=pl.BlockSpec((tm, tn), lambda i, j: (i, j)),
        out_shape=jax.ShapeDtypeStruct((m, f), out_dtype),
        compiler_params=_cparams("arbitrary", "arbitrary"),
        name="mm_glu",
    )(x, w3, w3)


def _mm_down_body(x_ref, w_ref, r_ref, o_ref):
    kk = pl.program_id(2)
    part = _dot(x_ref[...].astype(BF16), w_ref[...].astype(BF16))

    @pl.when(kk == 0)
    def _():
        o_ref[...] = part

    @pl.when(kk > 0)
    def _():
        o_ref[...] += part

    @pl.when(kk == pl.num_programs(2) - 1)
    def _():
        o_ref[...] = r_ref[...] + o_ref[...]


def _mm_down(x, w3, layer, resid, tm=2048, tn=512, tk=2048):
    m, k = x.shape
    n = w3.shape[2]
    tm, tn, tk = _tile(m, tm), _tile(n, tn), _tile(k, tk)
    return pl.pallas_call(
        _mm_down_body,
        grid=(m // tm, n // tn, k // tk),
        in_specs=[pl.BlockSpec((tm, tk), lambda i, j, kk: (i, kk)),
                  pl.BlockSpec((None, tk, tn), lambda i, j, kk: (layer, kk, j)),
                  pl.BlockSpec((tm, tn), lambda i, j, kk: (i, j))],
        out_specs=pl.BlockSpec((tm, tn), lambda i, j, kk: (i, j)),
        out_shape=jax.ShapeDtypeStruct((m, n), F32),
        compiler_params=_cparams("arbitrary", "arbitrary", "arbitrary"),
        name="mm_down",
    )(x, w3, resid)


def _conv_prompt_body(x_ref, wb_ref, wc_ref, wh_ref, cw_ref, y_ref, tail_ref, carry_ref, *, tiles_per_seq):
    i, j = pl.program_id(0), pl.program_id(1)
    x = x_ref[...]
    b = _dot(x, wb_ref[...].astype(BF16))
    c = _dot(x, wc_ref[...].astype(BF16))
    hv = _dot(x, wh_ref[...].astype(BF16))
    u = c * hv
    tm = u.shape[0]
    @pl.when(i % tiles_per_seq == 0)
    def _():
        carry_ref[j] = jnp.zeros(carry_ref.shape[1:], F32)

    prev = carry_ref[j]
    row = lax.broadcasted_iota(jnp.int32, u.shape, 0)
    u1 = jnp.where(row == 0, prev[1:2], pltpu.roll(u, 1, 0))
    u2 = jnp.where(row == 0, prev[0:1], jnp.where(row == 1, prev[1:2], pltpu.roll(u, 2, 0)))
    cw = cw_ref[...]
    y = cw[0:1] * u2 + cw[1:2] * u1 + cw[2:3] * u
    y_ref[...] = (b * y).astype(y_ref.dtype)
    carry_ref[j] = u[tm - 2:tm]
    tail_ref[...] = u[tm - 2:tm]


def _conv_prompt(xn, w_in_a, conv_w, layer, batch, seq, tm=1024, tn=256):
    m, d = xn.shape
    cd = conv_w.shape[2]
    tm, tn = _tile(seq, tm), _tile(cd, tn)
    nj = cd // tn
    tps = seq // tm
    y, tail = pl.pallas_call(
        functools.partial(_conv_prompt_body, tiles_per_seq=tps),
        grid=(m // tm, nj),
        in_specs=[pl.BlockSpec((tm, d), lambda i, j: (i, 0)),
                  pl.BlockSpec((None, d, tn), lambda i, j: (layer, 0, j)),
                  pl.BlockSpec((None, d, tn), lambda i, j: (layer, 0, nj + j)),
                  pl.BlockSpec((None, d, tn), lambda i, j: (layer, 0, 2 * nj + j)),
                  pl.BlockSpec((None, 3, tn), lambda i, j: (layer, 0, j))],
        out_specs=[pl.BlockSpec((tm, tn), lambda i, j: (i, j)),
                   pl.BlockSpec((None, 2, tn), lambda i, j: (i, 0, j))],
        out_shape=[jax.ShapeDtypeStruct((m, cd), BF16), jax.ShapeDtypeStruct((m // tm, 2, cd), F32)],
        scratch_shapes=[pltpu.VMEM((nj, 2, tn), F32)],
        compiler_params=_cparams("arbitrary", "arbitrary"),
        name="conv_prompt",
    )(xn, w_in_a, w_in_a, w_in_a, conv_w)
    return y, tail[tps - 1::tps]


def _conv_step_body(z_ref, s_ref, cw_ref, y_ref, ns_ref, *, cd):
    b, c, hv = z_ref[:, 0:cd], z_ref[:, cd:2 * cd], z_ref[:, 2 * cd:3 * cd]
    u = c * hv
    s0, s1 = s_ref[:, 0, :], s_ref[:, 1, :]
    cw = cw_ref[...]
    y = cw[0:1] * s0 + cw[1:2] * s1 + cw[2:3] * u
    y_ref[...] = b * y
    ns_ref[:, 0, :] = s1
    ns_ref[:, 1, :] = u


def _conv_step(z, state, cw):
    db, _, cd = state.shape
    return pl.pallas_call(
        functools.partial(_conv_step_body, cd=cd),
        out_shape=[jax.ShapeDtypeStruct((db, cd), F32), jax.ShapeDtypeStruct((db, 2, cd), F32)],
        compiler_params=pltpu.CompilerParams(vmem_limit_bytes=VMEM_LIMIT),
        name="conv_step",
    )(z, state, cw)


def _mem_attend_body(q_ref, g_ref, k_ref, v_ref, o_ref, *, heads):
    hd = q_ref.shape[-1] // heads
    scale = hd ** -0.5
    for h in range(heads):
        sl = slice(h * hd, (h + 1) * hd)
        q = _rms(q_ref[:, sl].astype(F32), g_ref[...]).astype(BF16)
        s = _dot_nt(q, k_ref[:, sl].astype(BF16)) * scale
        s = s - jnp.max(s, axis=-1, keepdims=True)
        p = jnp.exp(s)
        p = p / jnp.sum(p, axis=-1, keepdims=True)
        o_ref[:, sl] = _dot(p.astype(BF16), v_ref[:, sl].astype(BF16)).astype(o_ref.dtype)


def _mem_attend(q3, gain, k3, v3, heads, out_dtype, tq=512):
    b, t, w = q3.shape
    ml = k3.shape[1]
    tq = _tile(t, tq)
    hd = w // heads
    return pl.pallas_call(
        functools.partial(_mem_attend_body, heads=heads),
        grid=(b, t // tq),
        in_specs=[pl.BlockSpec((None, tq, w), lambda i, j: (i, j, 0)),
                  pl.BlockSpec((1, hd), lambda i, j: (0, 0)),
                  pl.BlockSpec((None, ml, w), lambda i, j: (i, 0, 0)),
                  pl.BlockSpec((None, ml, w), lambda i, j: (i, 0, 0))],
        out_specs=pl.BlockSpec((None, tq, w), lambda i, j: (i, j, 0)),
        out_shape=jax.ShapeDtypeStruct((b, t, w), out_dtype),
        compiler_params=_cparams("arbitrary", "arbitrary"),
        name="mem_attend",
    )(q3, gain.reshape(1, hd), k3, v3)


def _log_sigmoid(x):
    return -(jnp.maximum(-x, 0.0) + jnp.log1p(jnp.exp(-jnp.abs(x))))


def _logf_body(x_ref, w_ref, b_ref, o_ref, opad_ref):
    z = _dot(x_ref[...].astype(BF16), w_ref[...].astype(BF16))
    lf = _log_sigmoid(z + b_ref[...])
    opad_ref[...] = lf
    o_ref[...] = lf[:, :o_ref.shape[1]]


def _logf(x, w_pad, b_pad, h, tm=1024):
    m, k = x.shape
    tm = _tile(m, tm)
    return pl.pallas_call(
        _logf_body,
        grid=(m // tm,),
        in_specs=[pl.BlockSpec((tm, k), lambda i: (i, 0)),
                  pl.BlockSpec((k, LANE), lambda i: (0, 0)),
                  pl.BlockSpec((1, LANE), lambda i: (0, 0))],
        out_specs=[pl.BlockSpec((tm, h), lambda i: (i, 0)), pl.BlockSpec((tm, LANE), lambda i: (i, 0))],
        out_shape=[jax.ShapeDtypeStruct((m, h), F32), jax.ShapeDtypeStruct((m, LANE), F32)],
        compiler_params=_cparams("arbitrary"),
        name="logf",
    )(x, w_pad, b_pad)


def _cumsum_body(x_ref, o_ref):
    x = x_ref[...]
    t = x.shape[0]
    row = lax.broadcasted_iota(jnp.int32, x.shape, 0)
    s = 1
    while s < t:
        x = x + jnp.where(row >= s, pltpu.roll(x, s, 0), 0.0)
        s *= 2
    o_ref[...] = x


def _cumsum_time(x3):
    b, t, w = x3.shape
    return pl.pallas_call(
        _cumsum_body,
        grid=(b,),
        in_specs=[pl.BlockSpec((None, t, w), lambda i: (i, 0, 0))],
        out_specs=pl.BlockSpec((None, t, w), lambda i: (i, 0, 0)),
        out_shape=jax.ShapeDtypeStruct((b, t, w), F32),
        compiler_params=_cparams("arbitrary"),
        name="cumsum_time",
    )(x3)


def _fox_prompt_body(q_ref, k_ref, vt_ref, cq_ref, ck_ref, o_ref, ckb_ref, *, group, hd, tk):
    qi = pl.program_id(2)
    tq = q_ref.shape[0]
    cols = group * tq
    scale = hd ** -0.5
    q3 = jnp.concatenate([q_ref[:, g * hd:(g + 1) * hd] for g in range(group)], axis=0)
    cq = jnp.concatenate([cq_ref[g:g + 1, :] for g in range(group)], axis=1)

    @pl.when(qi == 0)
    def _():
        for g in range(group):
            ckb_ref[g] = jnp.broadcast_to(ck_ref[:, g:g + 1], ckb_ref.shape[1:])

    qpos = qi * tq + jnp.concatenate([lax.broadcasted_iota(jnp.int32, (tk, tq), 1)] * group, axis=1)
    krow = lax.broadcasted_iota(jnp.int32, (tk, cols), 0)

    def qk(ki):
        off = pl.multiple_of(ki * tk, tk)
        return _dot_nt(k_ref[pl.ds(off, tk), :].astype(BF16), q3)

    def softmax(ki, qk_raw, m, l, masked):
        off = pl.multiple_of(ki * tk, tk)
        ckb = jnp.concatenate([ckb_ref[g, pl.ds(off, tk), :] for g in range(group)], axis=1)
        s = qk_raw * scale + (cq - ckb)
        if masked:
            s = jnp.where(off + krow <= qpos, s, NEG_INF)
        m_new = jnp.maximum(m, jnp.max(s, axis=0, keepdims=True))
        alpha = jnp.exp(m - m_new)
        p = jnp.exp(s - m_new)
        l = alpha * l + jnp.sum(p, axis=0, keepdims=True)
        return m_new, l, alpha, p.astype(BF16)

    def pv(ki, p, alpha, acc):
        off = pl.multiple_of(ki * tk, tk)
        return alpha * acc + _dot(vt_ref[:, pl.ds(off, tk)].astype(BF16), p)

    def body(ki, carry):
        m, l, acc, qk_raw, p_prev, alpha_prev = carry
        qk_next = qk(ki + 1)
        acc = pv(jnp.maximum(ki - 1, 0), p_prev, alpha_prev, acc)
        m, l, alpha, p = softmax(ki, qk_raw, m, l, False)
        return m, l, acc, qk_next, p, alpha

    init = (jnp.full((1, cols), NEG_INF, F32), jnp.zeros((1, cols), F32), jnp.zeros((hd, cols), F32),
            qk(0), jnp.zeros((tk, cols), BF16), jnp.ones((1, cols), F32))
    n_full = (qi * tq) // tk
    m, l, acc, qk_raw, p_prev, alpha_prev = lax.fori_loop(0, n_full, body, init)
    acc = pv(jnp.maximum(n_full - 1, 0), p_prev, alpha_prev, acc)
    m, l, alpha, p = softmax(n_full, qk_raw, m, l, True)
    acc = pv(n_full, p, alpha, acc)
    out_t = acc / l
    for g in range(group):
        o_ref[:, g * hd:(g + 1) * hd] = out_t[:, g * tq:(g + 1) * tq].T.astype(o_ref.dtype)


def _fox_prompt(qn, k2, vt, c_col, c_row, batch, seq, kvh, group, hd, tk=256):
    m = qn.shape[0]
    tq = _tile(seq, LANE)
    tk = _tile(seq, tk)
    assert tk % tq == 0
    nq = seq // tq
    gw = group * hd
    return pl.pallas_call(
        functools.partial(_fox_prompt_body, group=group, hd=hd, tk=tk),
        grid=(batch, kvh, nq),
        in_specs=[pl.BlockSpec((tq, gw), lambda b, h, i: (b * nq + i, h)),
                  pl.BlockSpec((seq, hd), lambda b, h, i: (b, h)),
                  pl.BlockSpec((None, None, hd, seq), lambda b, h, i: (b, h, 0, 0)),
                  pl.BlockSpec((None, None, group, tq), lambda b, h, i: (b, h, 0, i)),
                  pl.BlockSpec((None, None, seq, group), lambda b, h, i: (b, h, 0, 0))],
        out_specs=pl.BlockSpec((tq, gw), lambda b, h, i: (b * nq + i, h)),
        out_shape=jax.ShapeDtypeStruct((m, kvh * gw), BF16),
        scratch_shapes=[pltpu.VMEM((group, seq, tq), F32)],
        compiler_params=_cparams("arbitrary", "arbitrary", "arbitrary"),
        name="fox_prompt",
    )(qn, k2, vt, c_row, c_col)


def _fox_decode_body(pt_ref, q_ref, kn_ref, vn_ref, lfn_ref, *refs, kvh, group, hd, pages):
    k_refs, v_refs, lf_refs = refs[:pages], refs[pages:2 * pages], refs[2 * pages:3 * pages]
    o_ref, m_ref, l_ref, acc_ref, d_ref = refs[3 * pages:]
    p = pl.program_id(1)
    heads = kvh * group
    ps = lf_refs[0].shape[1]
    scale = hd ** -0.5
    head_kv = lax.broadcasted_iota(jnp.int32, (heads, hd), 0) // group
    q = q_ref[...]

    def per_head(row_ref):
        out = jnp.zeros((heads, hd), F32)
        for j in range(kvh):
            out = out + jnp.where(head_kv == j, row_ref[:, j * hd:(j + 1) * hd], 0.0)
        return out

    @pl.when(p == 0)
    def _():
        m_ref[...] = jnp.sum(q.astype(F32) * per_head(kn_ref), axis=1, keepdims=True) * scale
        l_ref[...] = jnp.ones_like(l_ref)
        acc_ref[...] = per_head(vn_ref)
        d_ref[...] = lfn_ref[...]

    col_kv = lax.broadcasted_iota(jnp.int32, (heads, kvh * ps), 1) // ps
    own = col_kv == lax.broadcasted_iota(jnp.int32, (heads, kvh * ps), 0) // group
    lane = lax.broadcasted_iota(jnp.int32, (heads, ps), 1)
    qb = q.astype(BF16)
    for k_ref, v_ref, lf_ref in zip(k_refs, v_refs, lf_refs):
        lf = lf_ref[...]
        suf = lf
        s = 1
        while s < ps:
            suf = suf + jnp.where(lane < ps - s, pltpu.roll(suf, ps - s, 1), 0.0)
            s *= 2
        bias = d_ref[...] + (suf - lf)
        kcat = jnp.concatenate([k_ref[:, j, :] for j in range(kvh)], axis=0).astype(BF16)
        vcat = jnp.concatenate([v_ref[:, j, :] for j in range(kvh)], axis=0).astype(BF16)
        sc = _dot_nt(qb, kcat) * scale + jnp.concatenate([bias] * kvh, axis=1)
        sc = jnp.where(own, sc, NEG_INF)
        m_old = m_ref[...]
        m_new = jnp.maximum(m_old, jnp.max(sc, axis=-1, keepdims=True))
        alpha = jnp.exp(m_old - m_new)
        pr = jnp.exp(sc - m_new)
        l_ref[...] = alpha * l_ref[...] + jnp.sum(pr, axis=-1, keepdims=True)
        acc_ref[...] = alpha * acc_ref[...] + _dot(pr.astype(BF16), vcat)
        m_ref[...] = m_new
        d_ref[...] = d_ref[...] + suf[:, 0:1]

    @pl.when(p == pl.num_programs(1) - 1)
    def _():
        o_ref[...] = acc_ref[...] / l_ref[...]


def _fox_decode(page_table, q3, k_new, v_new, lf_new, cache_k, cache_v, cache_lf_t, kvh, group, hd, pages=4):
    db, heads, _ = q3.shape
    npg = page_table.shape[1]
    ps = cache_k.shape[1]
    w = kvh * hd
    while npg % pages:
        pages -= 1

    def page4(i):
        return lambda b, p, pt: (pt[b, npg - 1 - (p * pages + i)], 0, 0, 0)

    def page3(i):
        return lambda b, p, pt: (pt[b, npg - 1 - (p * pages + i)], 0, 0)

    grid_spec = pltpu.PrefetchScalarGridSpec(
        num_scalar_prefetch=1,
        grid=(db, npg // pages),
        in_specs=[pl.BlockSpec((None, heads, hd), lambda b, p, pt: (b, 0, 0)),
                  pl.BlockSpec((None, 1, w), lambda b, p, pt: (b, 0, 0)),
                  pl.BlockSpec((None, 1, w), lambda b, p, pt: (b, 0, 0)),
                  pl.BlockSpec((None, heads, 1), lambda b, p, pt: (b, 0, 0))]
                 + [pl.BlockSpec((None, ps, kvh, hd), page4(i)) for i in range(pages)]
                 + [pl.BlockSpec((None, ps, kvh, hd), page4(i)) for i in range(pages)]
                 + [pl.BlockSpec((None, heads, ps), page3(i)) for i in range(pages)],
        out_specs=pl.BlockSpec((None, heads, hd), lambda b, p, pt: (b, 0, 0)),
        scratch_shapes=[pltpu.VMEM((heads, 1), F32), pltpu.VMEM((heads, 1), F32),
                        pltpu.VMEM((heads, hd), F32), pltpu.VMEM((heads, 1), F32)],
    )
    return pl.pallas_call(
        functools.partial(_fox_decode_body, kvh=kvh, group=group, hd=hd, pages=pages),
        grid_spec=grid_spec,
        out_shape=jax.ShapeDtypeStruct((db, heads, hd), F32),
        compiler_params=_cparams("arbitrary", "arbitrary"),
        name="fox_decode",
    )(page_table, q3, k_new, v_new, lf_new, *([cache_k] * pages), *([cache_v] * pages), *([cache_lf_t] * pages))


def _router_body(x_ref, w_ref, b_ref, idx_ref, gate_ref, *, n_exp):
    logits = jnp.dot(x_ref[...], w_ref[...], preferred_element_type=F32,
                     precision=lax.Precision.HIGHEST) + b_ref[...]
    lane = lax.broadcasted_iota(jnp.int32, logits.shape, 1)
    logits = jnp.where(lane < n_exp, logits, NEG_INF)
    m1 = jnp.max(logits, axis=-1, keepdims=True)
    i1 = jnp.min(jnp.where(logits == m1, lane, LANE), axis=-1, keepdims=True)
    rest = jnp.where(lane == i1, NEG_INF, logits)
    m2 = jnp.max(rest, axis=-1, keepdims=True)
    i2 = jnp.min(jnp.where(rest == m2, lane, LANE), axis=-1, keepdims=True)
    e = jnp.exp(m2 - m1)
    g1 = 1.0 / (1.0 + e)
    g2 = e / (1.0 + e)
    idx_ref[...] = jnp.where(lane == 0, i1, jnp.where(lane == 1, i2, 0))
    gate_ref[...] = jnp.where(lane == 0, g1, jnp.where(lane == 1, g2, 0.0))


def _router(x, w_pad, b_pad, n_exp):
    m, k = x.shape
    tm = _tile(m, 256)
    return pl.pallas_call(
        functools.partial(_router_body, n_exp=n_exp),
        grid=(m // tm,),
        in_specs=[pl.BlockSpec((tm, k), lambda i: (i, 0)),
                  pl.BlockSpec((k, LANE), lambda i: (0, 0)),
                  pl.BlockSpec((1, LANE), lambda i: (0, 0))],
        out_specs=[pl.BlockSpec((tm, LANE), lambda i: (i, 0)), pl.BlockSpec((tm, LANE), lambda i: (i, 0))],
        out_shape=[jax.ShapeDtypeStruct((m, LANE), jnp.int32), jax.ShapeDtypeStruct((m, LANE), F32)],
        compiler_params=_cparams("arbitrary"),
        name="router",
    )(x, w_pad, b_pad)


def _gather_body(tok_ref, used_ref, a_ref, b_ref, o_ref, buf_ref, sem, *, tg):
    i = pl.program_id(0)
    ma = a_ref.shape[0]
    slot = i % 2

    def row_copy(src_ref, t, r, s):
        return pltpu.make_async_copy(src_ref.at[pl.ds(t, 1)], buf_ref.at[s, pl.ds(r, 1)], sem.at[s])

    def issue_tile(tile, s):
        base = tile * tg

        @pl.when(base < used_ref[0])
        def _():
            def issue(r, c):
                t = tok_ref[base + r]

                @pl.when(t < ma)
                def _():
                    row_copy(a_ref, t, r, s).start()

                @pl.when(t >= ma)
                def _():
                    row_copy(b_ref, t - ma, r, s).start()
                return c

            lax.fori_loop(0, tg, issue, 0, unroll=8)

    @pl.when(i == 0)
    def _():
        issue_tile(0, 0)

    @pl.when(i + 1 < pl.num_programs(0))
    def _():
        issue_tile(i + 1, 1 - slot)

    @pl.when(i * tg < used_ref[0])
    def _():
        def drain(r, c):
            row_copy(a_ref, 0, r, slot).wait()
            return c

        lax.fori_loop(0, tg, drain, 0, unroll=8)
        o_ref[...] = buf_ref[slot].astype(o_ref.dtype)

    @pl.when(i * tg >= used_ref[0])
    def _():
        o_ref[...] = jnp.zeros_like(o_ref)


def _gather_rows(row_token, used_rows, src_a, src_b, tg=256):
    p = row_token.shape[0]
    d = src_a.shape[1]
    tg = _tile(p, tg)
    grid_spec = pltpu.PrefetchScalarGridSpec(
        num_scalar_prefetch=2,
        grid=(p // tg,),
        in_specs=[pl.BlockSpec(memory_space=pl.ANY), pl.BlockSpec(memory_space=pl.ANY)],
        out_specs=pl.BlockSpec((tg, d), lambda i, tok, used: (i, 0)),
        scratch_shapes=[pltpu.VMEM((2, tg, d), F32), pltpu.SemaphoreType.DMA((2,))],
    )
    return pl.pallas_call(
        functools.partial(_gather_body, tg=tg),
        grid_spec=grid_spec,
        out_shape=jax.ShapeDtypeStruct((p, d), BF16),
        compiler_params=_cparams("arbitrary"),
        name="moe_gather",
    )(row_token, used_rows, src_a, src_b)


def _combine_body(pos_ref, y_ref, h_ref, g_ref, o_ref, buf_ref, sem, *, tt):
    i = pl.program_id(0)
    base = i * tt

    def row_copy(k, src_row, r):
        return pltpu.make_async_copy(y_ref.at[pl.ds(src_row, 1)], buf_ref.at[k, pl.ds(r, 1)], sem)

    def issue(r, c):
        row_copy(0, pos_ref[2 * (base + r)], r).start()
        row_copy(1, pos_ref[2 * (base + r) + 1], r).start()
        return c

    lax.fori_loop(0, tt, issue, 0)

    def drain(r, c):
        row_copy(0, 0, r).wait()
        row_copy(1, 0, r).wait()
        return c

    lax.fori_loop(0, tt, drain, 0)
    g = g_ref[...]
    o_ref[...] = h_ref[...] + (g[:, 0:1] * buf_ref[0] + g[:, 1:2] * buf_ref[1])


def _combine(pos_flat, y_sorted, h, gates, tt=256):
    m, d = h.shape
    tt = _tile(m, tt)
    grid_spec = pltpu.PrefetchScalarGridSpec(
        num_scalar_prefetch=1,
        grid=(m // tt,),
        in_specs=[pl.BlockSpec(memory_space=pl.ANY),
                  pl.BlockSpec((tt, d), lambda i, pos: (i, 0)),
                  pl.BlockSpec((tt, LANE), lambda i, pos: (i, 0))],
        out_specs=pl.BlockSpec((tt, d), lambda i, pos: (i, 0)),
        scratch_shapes=[pltpu.VMEM((2, tt, d), F32), pltpu.SemaphoreType.DMA(())],
    )
    return pl.pallas_call(
        functools.partial(_combine_body, tt=tt),
        grid_spec=grid_spec,
        out_shape=jax.ShapeDtypeStruct((m, d), F32),
        compiler_params=_cparams("arbitrary"),
        name="moe_combine",
    )(pos_flat, y_sorted, h, gates)


def _for_row_blocks(nrows, total, big, small, live, dead=None):
    def run(fn, size, lo, hi):
        def body(t, carry):
            fn(pl.ds(pl.multiple_of(t * size, size), size), size)
            return carry
        lax.fori_loop(lo, hi, body, 0)

    nbig = nrows // big
    run(live, big, 0, nbig)
    done = nbig * (big // small)
    end = done + (nrows - nbig * big + small - 1) // small
    run(live, small, done, end)
    if dead is not None:
        run(dead, small, end, total // small)


def _moe_glu_body(ce_ref, cx_ref, cv_ref, used_ref, x_ref, wg_ref, wu_ref, o_ref, wgs_ref, wus_ref, *, big, small):
    c = pl.program_id(0)

    @pl.when(c < used_ref[0])
    def _():
        wgs_ref[...] = wg_ref[...].astype(BF16)
        wus_ref[...] = wu_ref[...].astype(BF16)

        def live(rows, size):
            x = x_ref[rows, :]
            g = _dot(x, wgs_ref[...])
            u = _dot(x, wus_ref[...])
            o_ref[rows, :] = _silu_mul(g, u).astype(o_ref.dtype)

        def dead(rows, size):
            o_ref[rows, :] = jnp.zeros((size, o_ref.shape[1]), o_ref.dtype)

        _for_row_blocks(cv_ref[c], x_ref.shape[0], big, small, live, dead)

    @pl.when(c >= used_ref[0])
    def _():
        o_ref[...] = jnp.zeros_like(o_ref)


def _moe_glu(sched, x_sorted, we_gu, chunk, big, small, tn=256):
    ce, cx, cv, used = sched
    p, d = x_sorted.shape
    f = we_gu.shape[2] // 2
    tn = _tile(f, tn)
    nj = f // tn
    nc = p // chunk

    def jpin(c, j, used):
        return jnp.where(c < used[0], j, nj - 1)

    grid_spec = pltpu.PrefetchScalarGridSpec(
        num_scalar_prefetch=4,
        grid=(nc, nj),
        in_specs=[pl.BlockSpec((chunk, d), lambda c, j, ce, cx, cv, used: (cx[c], 0),
                               pipeline_mode=pl.Buffered(1)),
                  pl.BlockSpec((None, d, tn), lambda c, j, ce, cx, cv, used: (ce[c], 0, jpin(c, j, used))),
                  pl.BlockSpec((None, d, tn), lambda c, j, ce, cx, cv, used: (ce[c], 0, nj + jpin(c, j, used)))],
        out_specs=pl.BlockSpec((chunk, tn), lambda c, j, ce, cx, cv, used: (c, j)),
        scratch_shapes=[pltpu.VMEM((d, tn), BF16), pltpu.VMEM((d, tn), BF16)],
    )
    return pl.pallas_call(
        functools.partial(_moe_glu_body, big=big, small=small),
        grid_spec=grid_spec,
        out_shape=jax.ShapeDtypeStruct((p, f), BF16),
        compiler_params=_cparams("arbitrary", "arbitrary"),
        name="moe_glu",
    )(ce, cx, cv, used, x_sorted, we_gu, we_gu)


def _moe_down_body(ce_ref, cx_ref, cv_ref, used_ref, x_ref, w_ref, o_ref, ws_ref, *, big, small):
    c, kk = pl.program_id(0), pl.program_id(2)

    @pl.when(c < used_ref[0])
    def _():
        ws_ref[...] = w_ref[...].astype(BF16)

        def first(rows, size):
            o_ref[rows, :] = _dot(x_ref[rows, :], ws_ref[...])

        def accumulate(rows, size):
            o_ref[rows, :] += _dot(x_ref[rows, :], ws_ref[...])

        def dead(rows, size):
            o_ref[rows, :] = jnp.zeros((size, o_ref.shape[1]), o_ref.dtype)

        @pl.when(kk == 0)
        def _():
            _for_row_blocks(cv_ref[c], x_ref.shape[0], big, small, first, dead)

        @pl.when(kk > 0)
        def _():
            _for_row_blocks(cv_ref[c], x_ref.shape[0], big, small, accumulate)

    @pl.when((c >= used_ref[0]) & (kk == 0))
    def _():
        o_ref[...] = jnp.zeros_like(o_ref)


def _moe_down(sched, act_sorted, we_down, chunk, big, small, tn=512, tk=2048):
    ce, cx, cv, used = sched
    p, f = act_sorted.shape
    d = we_down.shape[2]
    tn, tk = _tile(d, tn), _tile(f, tk)
    nj, nk = d // tn, f // tk
    nc = p // chunk

    def pin(c, v, last, used):
        return jnp.where(c < used[0], v, last)

    grid_spec = pltpu.PrefetchScalarGridSpec(
        num_scalar_prefetch=4,
        grid=(nc, nj, nk),
        in_specs=[pl.BlockSpec((chunk, tk), lambda c, j, k, ce, cx, cv, used: (cx[c], pin(c, k, nk - 1, used))),
                  pl.BlockSpec((None, tk, tn), lambda c, j, k, ce, cx, cv, used:
                               (ce[c], pin(c, k, nk - 1, used), pin(c, j, nj - 1, used)))],
        out_specs=pl.BlockSpec((chunk, tn), lambda c, j, k, ce, cx, cv, used: (c, j)),
        scratch_shapes=[pltpu.VMEM((tk, tn), BF16)],
    )
    return pl.pallas_call(
        functools.partial(_moe_down_body, big=big, small=small),
        grid_spec=grid_spec,
        out_shape=jax.ShapeDtypeStruct((p, d), F32),
        compiler_params=_cparams("arbitrary", "arbitrary", "arbitrary"),
        name="moe_down",
    )(ce, cx, cv, used, act_sorted, we_down)


def _moe(h_p, h_s, hn_p, hn_s, w_router, b_router, we_gu, we_down):
    mp, d = hn_p.shape
    ms = hn_s.shape[0]
    n_exp = we_gu.shape[0]
    mt = mp + ms
    w_pad = jnp.pad(w_router, ((0, 0), (0, LANE - n_exp)))
    b_pad = jnp.pad(b_router.reshape(1, n_exp).astype(F32), ((0, 0), (0, LANE - n_exp)))
    idx_p, gate_p = _router(hn_p, w_pad, b_pad, n_exp)
    idx_s, gate_s = _router(hn_s, w_pad, b_pad, n_exp)

    chunk = min(MOE_CHUNK, -(-2 * mt // MOE_SMALL) * MOE_SMALL)
    big = min(MOE_BIG, chunk)
    idx = jnp.concatenate([idx_p[:, :2], idx_s[:, :2]], axis=0)
    sel = (idx[:, :, None] == jnp.arange(n_exp, dtype=jnp.int32)[None, None, :]).any(axis=1).astype(jnp.int32)
    rank = jnp.cumsum(sel, axis=0) - sel
    cnt = jnp.sum(sel, axis=0)
    nch = (cnt + chunk - 1) // chunk
    ch_end = jnp.cumsum(nch)
    ch_off = ch_end - nch
    n_chunks = (2 * mt + n_exp * (chunk - 1)) // chunk
    p_rows = n_chunks * chunk
    pos = jnp.take_along_axis(ch_off[None, :] * chunk + rank, idx, axis=1)
    tok = jnp.repeat(jnp.arange(mt, dtype=jnp.int32), 2)
    row_token = jnp.zeros((p_rows,), jnp.int32).at[pos.reshape(-1)].set(tok)
    used = ch_end[-1].astype(jnp.int32)
    cid = jnp.minimum(jnp.arange(n_chunks, dtype=jnp.int32), used - 1)
    ce = jnp.minimum(jnp.sum((ch_end[None, :] <= cid[:, None]).astype(jnp.int32), axis=1), n_exp - 1)
    cv = jnp.clip(cnt[ce] - (cid - ch_off[ce]) * chunk, 0, chunk).astype(jnp.int32)
    sched = (ce, cid, cv, used.reshape(1))

    x_sorted = _gather_rows(row_token, (used * chunk).reshape(1), hn_p, hn_s)
    act = _moe_glu(sched, x_sorted, we_gu, chunk, big, MOE_SMALL)
    y_sorted = _moe_down(sched, act, we_down, chunk, big, MOE_SMALL)
    pos = pos.astype(jnp.int32)
    out_p = _combine(pos[:mp].reshape(-1), y_sorted, h_p, gate_p)
    out_s = _combine(pos[mp:].reshape(-1), y_sorted, h_s, gate_s)
    return out_p, out_s


def kernel(x_prompt, x_sample, mem_prompt, cache_k, cache_v, cache_logf, state_conv, cache_mem_k, cache_mem_v, page_table, norm_mix, norm_ffn, w_in_a, conv_w, w_in_b, q_norm_b, w_out, w_mem_kv, mem_q_norm, mem_k_norm, kv_norm, w_kv, b_f, k_norm, w_gu, w_down, w_router, b_router, we_gu, we_down):
    bsz, seq, d = x_prompt.shape
    db, dec_seq, _ = x_sample.shape
    assert dec_seq == 1, "the decode path handles one new token per sequence"
    depth, _, ml, mem_heads, mem_hd = cache_mem_k.shape
    assert depth == 2 and w_in_a.shape[0] == 1 and w_in_b.shape[0] == 1
    mem_w = mem_heads * mem_hd
    n_pool, ps, kvh, hd = cache_k.shape
    heads = cache_logf.shape[2]
    group = heads // kvh
    kv_w = kvh * hd
    cd = conv_w.shape[2]
    assert conv_w.shape[1] == 3 and state_conv.shape[2] == 2
    mp = bsz * seq

    mem_bf = mem_prompt.reshape(bsz * ml, d).astype(BF16)
    mk, mv = [], []
    for i in range(depth):
        mk.append(_mm_headnorm(mem_bf, w_mem_kv, i, 0, mem_w, mem_k_norm[i], mem_hd, F32))
        mv.append(_mm(mem_bf, w_mem_kv, i, mem_w, mem_w, F32))
    mem_k_prompt = jnp.stack(mk).reshape(depth, bsz, ml, mem_heads, mem_hd)
    mem_v_prompt = jnp.stack(mv).reshape(depth, bsz, ml, mem_heads, mem_hd)

    w_kv3 = w_kv.reshape(1, d, -1)
    w_lf = jnp.pad(w_kv[:, 2 * kv_w:], ((0, 0), (0, LANE - heads)))
    b_lf = jnp.pad(b_f.reshape(1, heads).astype(F32), ((0, 0), (0, LANE - heads)))
    we_gu3 = we_gu.reshape(we_gu.shape[1:])
    we_down3 = we_down.reshape(we_down.shape[1:])

    def layer0_tail(h0, main, q_mem, mem_k0, mem_v0, nb, act_dtype):
        t = h0.shape[0] // nb
        mo = _mem_attend(q_mem.reshape(nb, t, mem_w), mem_q_norm[0], mem_k0.reshape(nb, ml, mem_w),
                         mem_v0.reshape(nb, ml, mem_w), mem_heads, act_dtype).reshape(nb * t, mem_w)
        h1 = _mm_out(main, mo, w_out, 0, h0)
        (hn,) = _rmsnorm(h1, norm_ffn[0], (act_dtype,))
        act = _mm_glu(hn, w_gu, 0, act_dtype)
        h2 = _mm_down(act, w_down, 0, h1)
        (kvn,) = _rmsnorm(h2, kv_norm, (act_dtype,))
        k = _mm_headnorm(kvn, w_kv3, 0, 0, kv_w, k_norm, hd, F32)
        v = _mm(kvn, w_kv3, 0, kv_w, kv_w, F32)
        logf, logf_pad = _logf(kvn, w_lf, b_lf, heads)
        (xn1,) = _rmsnorm(h2, norm_mix[1], (act_dtype,))
        qn = _mm_headnorm(xn1, w_in_b, 0, 0, heads * hd, q_norm_b[0], hd, act_dtype)
        q_mem1 = _mm(xn1, w_in_b, 0, heads * hd, mem_w, F32)
        return h2, k, v, logf, logf_pad, qn, q_mem1

    def layer1_attn_out(h2, attn, q_mem1, mem_k1, mem_v1, nb, act_dtype):
        t = h2.shape[0] // nb
        mo = _mem_attend(q_mem1.reshape(nb, t, mem_w), mem_q_norm[1], mem_k1.reshape(nb, ml, mem_w),
                         mem_v1.reshape(nb, ml, mem_w), mem_heads, act_dtype).reshape(nb * t, mem_w)
        h3 = _mm_out(attn, mo, w_out, 1, h2)
        (hn3,) = _rmsnorm(h3, norm_ffn[1], (F32,))
        return h3, hn3

    xp = x_prompt.reshape(mp, d)
    (xn,) = _rmsnorm(xp, norm_mix[0], (BF16,))
    main_p, conv_tail = _conv_prompt(xn, w_in_a, conv_w, 0, bsz, seq)
    q_mem_p = _mm(xn, w_in_a, 0, 3 * cd, mem_w, F32)
    h2_p, k_p, v_p, logf_p, logf_pad_p, qn_p, q_mem1_p = layer0_tail(
        xp, main_p, q_mem_p, mk[0], mv[0], bsz, BF16)
    c = _cumsum_time(logf_pad_p.reshape(bsz, seq, LANE))[:, :, :heads].reshape(bsz, seq, kvh, group)
    vt_p = v_p.reshape(bsz, seq, kvh, hd).transpose(0, 2, 3, 1)
    attn_p = _fox_prompt(qn_p, k_p, vt_p, c.transpose(0, 2, 1, 3), c.transpose(0, 2, 3, 1),
                         bsz, seq, kvh, group, hd)
    h3_p, hn3_p = layer1_attn_out(h2_p, attn_p, q_mem1_p, mk[1], mv[1], bsz, BF16)

    xs = x_sample.reshape(db, d)
    (xn_s,) = _rmsnorm(xs, norm_mix[0], (F32,))
    z_s = _mm(xn_s, w_in_a, 0, 0, 3 * cd + mem_w, F32)
    main_s, conv_new_s = _conv_step(z_s, state_conv[0], conv_w[0])
    h2_s, k_s, v_s, logf_s, _, qn_s, q_mem1_s = layer0_tail(
        xs, main_s, z_s[:, 3 * cd:], cache_mem_k[0], cache_mem_v[0], db, F32)
    attn_s = _fox_decode(page_table, qn_s.reshape(db, heads, hd), k_s.reshape(db, 1, kv_w),
                         v_s.reshape(db, 1, kv_w), logf_s.reshape(db, heads, 1),
                         cache_k, cache_v, cache_logf.transpose(0, 2, 1), kvh, group, hd).reshape(db, heads * hd)
    h3_s, hn3_s = layer1_attn_out(h2_s, attn_s, q_mem1_s, cache_mem_k[1], cache_mem_v[1], db, F32)

    y_p, y_s = _moe(h3_p, h3_s, hn3_p, hn3_s, w_router[0], b_router[0], we_gu3, we_down3)

    return (y_p.reshape(bsz, seq, d), y_s.reshape(db, 1, d),
            k_p.reshape(bsz, seq, kvh, hd), v_p.reshape(bsz, seq, kvh, hd), logf_p.reshape(bsz, seq, heads),
            conv_tail.reshape(1, bsz, 2, cd), mem_k_prompt, mem_v_prompt,
            k_s.reshape(db, 1, kvh, hd), v_s.reshape(db, 1, kvh, hd), logf_s.reshape(db, 1, heads),
            conv_new_s.reshape(1, db, 2, cd))
```

```python
import functools

import jax
import jax.numpy as jnp
from jax import lax
from jax.experimental import pallas as pl
from jax.experimental.pallas import tpu as pltpu

F32 = jnp.float32
BF16 = jnp.bfloat16
EPS = 1e-6
NEG_INF = float("-inf")

V7X_VMEM_BYTES = 64 * 1024 * 1024
VMEM_LIMIT = V7X_VMEM_BYTES - 8 * 1024 * 1024
LANE = 128
MOE_CHUNK = 2560
MOE_BIG = 512
MOE_SMALL = 128


def _cparams(*sem):
    return pltpu.CompilerParams(dimension_semantics=sem, vmem_limit_bytes=VMEM_LIMIT)


def _tile(n, pref):
    if n <= pref:
        return n
    t = pref
    while n % t:
        t -= LANE
    assert t > 0
    return t


def _dot(a, b):
    return jnp.dot(a, b, preferred_element_type=F32)


def _dot_nt(a, b):
    return lax.dot_general(a, b, (((1,), (1,)), ((), ())), preferred_element_type=F32)


def _rms(x, g):
    return x * lax.rsqrt(jnp.mean(x * x, axis=-1, keepdims=True) + EPS) * g


def _rmsnorm_body(x_ref, g_ref, *o_refs):
    y = _rms(x_ref[...].astype(F32), g_ref[...])
    for o in o_refs:
        o[...] = y.astype(o.dtype)


def _rmsnorm(x, g, dtypes):
    m, d = x.shape
    tr = _tile(m, 256)
    outs = pl.pallas_call(
        _rmsnorm_body,
        grid=(m // tr,),
        in_specs=[pl.BlockSpec((tr, d), lambda i: (i, 0)), pl.BlockSpec((1, d), lambda i: (0, 0))],
        out_specs=[pl.BlockSpec((tr, d), lambda i: (i, 0)) for _ in dtypes],
        out_shape=[jax.ShapeDtypeStruct((m, d), dt) for dt in dtypes],
        compiler_params=_cparams("arbitrary"),
        name="rmsnorm",
    )(x, g.reshape(1, d))
    return outs


def _mm_body(x_ref, w_ref, o_ref):
    o_ref[...] = _dot(x_ref[...].astype(BF16), w_ref[...].astype(BF16)).astype(o_ref.dtype)


def _mm(x, w3, layer, c0, n, out_dtype, tm=1024, tn=512):
    m, k = x.shape
    tm, tn = _tile(m, tm), _tile(n, tn)
    assert c0 % tn == 0
    return pl.pallas_call(
        _mm_body,
        grid=(m // tm, n // tn),
        in_specs=[pl.BlockSpec((tm, k), lambda i, j: (i, 0)),
                  pl.BlockSpec((None, k, tn), lambda i, j: (layer, 0, c0 // tn + j))],
        out_specs=pl.BlockSpec((tm, tn), lambda i, j: (i, j)),
        out_shape=jax.ShapeDtypeStruct((m, n), out_dtype),
        compiler_params=_cparams("arbitrary", "arbitrary"),
        name="mm",
    )(x, w3)


def _mm_headnorm_body(x_ref, w_ref, g_ref, o_ref, *, hd):
    z = _dot(x_ref[...].astype(BF16), w_ref[...].astype(BF16))
    for h in range(z.shape[1] // hd):
        o_ref[:, h * hd:(h + 1) * hd] = _rms(z[:, h * hd:(h + 1) * hd], g_ref[...]).astype(o_ref.dtype)


def _mm_headnorm(x, w3, layer, c0, n, gain, hd, out_dtype, tm=1024, tn=512):
    m, k = x.shape
    tm, tn = _tile(m, tm), _tile(n, tn)
    assert c0 % tn == 0 and tn % hd == 0
    return pl.pallas_call(
        functools.partial(_mm_headnorm_body, hd=hd),
        grid=(m // tm, n // tn),
        in_specs=[pl.BlockSpec((tm, k), lambda i, j: (i, 0)),
                  pl.BlockSpec((None, k, tn), lambda i, j: (layer, 0, c0 // tn + j)),
                  pl.BlockSpec((1, hd), lambda i, j: (0, 0))],
        out_specs=pl.BlockSpec((tm, tn), lambda i, j: (i, j)),
        out_shape=jax.ShapeDtypeStruct((m, n), out_dtype),
        compiler_params=_cparams("arbitrary", "arbitrary"),
        name="mm_headnorm",
    )(x, w3, gain.reshape(1, hd))


def _mm_out_body(x1_ref, x2_ref, w_ref, r_ref, o_ref):
    k1 = x1_ref.shape[1]
    w = w_ref[...].astype(BF16)
    acc = _dot(x1_ref[...].astype(BF16), w[:k1]) + _dot(x2_ref[...].astype(BF16), w[k1:])
    o_ref[...] = r_ref[...] + acc


def _mm_out(x1, x2, w3, layer, resid, tm=1024, tn=512):
    m, k1 = x1.shape
    k2 = x2.shape[1]
    n = w3.shape[2]
    tm, tn = _tile(m, tm), _tile(n, tn)
    return pl.pallas_call(
        _mm_out_body,
        grid=(m // tm, n // tn),
        in_specs=[pl.BlockSpec((tm, k1), lambda i, j: (i, 0)),
                  pl.BlockSpec((tm, k2), lambda i, j: (i, 0)),
                  pl.BlockSpec((None, k1 + k2, tn), lambda i, j: (layer, 0, j)),
                  pl.BlockSpec((tm, tn), lambda i, j: (i, j))],
        out_specs=pl.BlockSpec((tm, tn), lambda i, j: (i, j)),
        out_shape=jax.ShapeDtypeStruct((m, n), F32),
        compiler_params=_cparams("arbitrary", "arbitrary"),
        name="mm_out",
    )(x1, x2, w3, resid)


def _silu_mul(g, u):
    return g * jax.nn.sigmoid(g) * u


def _mm_glu_body(x_ref, wg_ref, wu_ref, o_ref):
    x = x_ref[...].astype(BF16)
    g = _dot(x, wg_ref[...].astype(BF16))
    u = _dot(x, wu_ref[...].astype(BF16))
    o_ref[...] = _silu_mul(g, u).astype(o_ref.dtype)


def _mm_glu(x, w3, layer, out_dtype, tm=1024, tn=256):
    m, k = x.shape
    f = w3.shape[2] // 2
    tm, tn = _tile(m, tm), _tile(f, tn)
    return pl.pallas_call(
        _mm_glu_body,
        grid=(m // tm, f // tn),
        in_specs=[pl.BlockSpec((tm, k), lambda i, j: (i, 0)),
                  pl.BlockSpec((None, k, tn), lambda i, j: (layer, 0, j)),
                  pl.BlockSpec((None, k, tn), lambda i, j: (layer, 0, f // tn + j))],
        out_specs=pl.BlockSpec((tm, tn), lambda i, j: (i, j)),
        out_shape=jax.ShapeDtypeStruct((m, f), out_dtype),
        compiler_params=_cparams("arbitrary", "arbitrary"),
        name="mm_glu",
    )(x, w3, w3)


def _mm_down_body(x_ref, w_ref, r_ref, o_ref):
    kk = pl.program_id(2)
    part = _dot(x_ref[...].astype(BF16), w_ref[...].astype(BF16))

    @pl.when(kk == 0)
    def _():
        o_ref[...] = part

    @pl.when(kk > 0)
    def _():
        o_ref[...] += part

    @pl.when(kk == pl.num_programs(2) - 1)
    def _():
        o_ref[...] = r_ref[...] + o_ref[...]


def _mm_down(x, w3, layer, resid, tm=2048, tn=512, tk=2048):
    m, k = x.shape
    n = w3.shape[2]
    tm, tn, tk = _tile(m, tm), _tile(n, tn), _tile(k, tk)
    return pl.pallas_call(
        _mm_down_body,
        grid=(m // tm, n // tn, k // tk),
        in_specs=[pl.BlockSpec((tm, tk), lambda i, j, kk: (i, kk)),
                  pl.BlockSpec((None, tk, tn), lambda i, j, kk: (layer, kk, j)),
                  pl.BlockSpec((tm, tn), lambda i, j, kk: (i, j))],
        out_specs=pl.BlockSpec((tm, tn), lambda i, j, kk: (i, j)),
        out_shape=jax.ShapeDtypeStruct((m, n), F32),
        compiler_params=_cparams("arbitrary", "arbitrary", "arbitrary"),
        name="mm_down",
    )(x, w3, resid)


def _conv_prompt_body(x_ref, wb_ref, wc_ref, wh_ref, cw_ref, y_ref, tail_ref, carry_ref, *, tiles_per_seq):
    i, j = pl.program_id(0), pl.program_id(1)
    x = x_ref[...]
    b = _dot(x, wb_ref[...].astype(BF16))
    c = _dot(x, wc_ref[...].astype(BF16))
    hv = _dot(x, wh_ref[...].astype(BF16))
    u = c * hv
    tm = u.shape[0]
    @pl.when(i % tiles_per_seq == 0)
    def _():
        carry_ref[j] = jnp.zeros(carry_ref.shape[1:], F32)

    prev = carry_ref[j]
    row = lax.broadcasted_iota(jnp.int32, u.shape, 0)
    u1 = jnp.where(row == 0, prev[1:2], pltpu.roll(u, 1, 0))
    u2 = jnp.where(row == 0, prev[0:1], jnp.where(row == 1, prev[1:2], pltpu.roll(u, 2, 0)))
    cw = cw_ref[...]
    y = cw[0:1] * u2 + cw[1:2] * u1 + cw[2:3] * u
    y_ref[...] = (b * y).astype(y_ref.dtype)
    carry_ref[j] = u[tm - 2:tm]
    tail_ref[...] = u[tm - 2:tm]


def _conv_prompt(xn, w_in_a, conv_w, layer, batch, seq, tm=1024, tn=256):
    m, d = xn.shape
    cd = conv_w.shape[2]
    tm, tn = _tile(seq, tm), _tile(cd, tn)
    nj = cd // tn
    tps = seq // tm
    y, tail = pl.pallas_call(
        functools.partial(_conv_prompt_body, tiles_per_seq=tps),
        grid=(m // tm, nj),
        in_specs=[pl.BlockSpec((tm, d), lambda i, j: (i, 0)),
                  pl.BlockSpec((None, d, tn), lambda i, j: (layer, 0, j)),
                  pl.BlockSpec((None, d, tn), lambda i, j: (layer, 0, nj + j)),
                  pl.BlockSpec((None, d, tn), lambda i, j: (layer, 0, 2 * nj + j)),
                  pl.BlockSpec((None, 3, tn), lambda i, j: (layer, 0, j))],
        out_specs=[pl.BlockSpec((tm, tn), lambda i, j: (i, j)),
                   pl.BlockSpec((None, 2, tn), lambda i, j: (i, 0, j))],
        out_shape=[jax.ShapeDtypeStruct((m, cd), BF16), jax.ShapeDtypeStruct((m // tm, 2, cd), F32)],
        scratch_shapes=[pltpu.VMEM((nj, 2, tn), F32)],
        compiler_params=_cparams("arbitrary", "arbitrary"),
        name="conv_prompt",
    )(xn, w_in_a, w_in_a, w_in_a, conv_w)
    return y, tail[tps - 1::tps]


def _conv_step_body(z_ref, s_ref, cw_ref, y_ref, ns_ref, *, cd):
    b, c, hv = z_ref[:, 0:cd], z_ref[:, cd:2 * cd], z_ref[:, 2 * cd:3 * cd]
    u = c * hv
    s0, s1 = s_ref[:, 0, :], s_ref[:, 1, :]
    cw = cw_ref[...]
    y = cw[0:1] * s0 + cw[1:2] * s1 + cw[2:3] * u
    y_ref[...] = b * y
    ns_ref[:, 0, :] = s1
    ns_ref[:, 1, :] = u


def _conv_step(z, state, cw):
    db, _, cd = state.shape
    return pl.pallas_call(
        functools.partial(_conv_step_body, cd=cd),
        out_shape=[jax.ShapeDtypeStruct((db, cd), F32), jax.ShapeDtypeStruct((db, 2, cd), F32)],
        compiler_params=pltpu.CompilerParams(vmem_limit_bytes=VMEM_LIMIT),
        name="conv_step",
    )(z, state, cw)


def _mem_attend_body(q_ref, g_ref, k_ref, v_ref, o_ref, *, heads):
    hd = q_ref.shape[-1] // heads
    scale = hd ** -0.5
    for h in range(heads):
        sl = slice(h * hd, (h + 1) * hd)
        q = _rms(q_ref[:, sl].astype(F32), g_ref[...]).astype(BF16)
        s = _dot_nt(q, k_ref[:, sl].astype(BF16)) * scale
        s = s - jnp.max(s, axis=-1, keepdims=True)
        p = jnp.exp(s)
        p = p / jnp.sum(p, axis=-1, keepdims=True)
        o_ref[:, sl] = _dot(p.astype(BF16), v_ref[:, sl].astype(BF16)).astype(o_ref.dtype)


def _mem_attend(q3, gain, k3, v3, heads, out_dtype, tq=512):
    b, t, w = q3.shape
    ml = k3.shape[1]
    tq = _tile(t, tq)
    hd = w // heads
    return pl.pallas_call(
        functools.partial(_mem_attend_body, heads=heads),
        grid=(b, t // tq),
        in_specs=[pl.BlockSpec((None, tq, w), lambda i, j: (i, j, 0)),
                  pl.BlockSpec((1, hd), lambda i, j: (0, 0)),
                  pl.BlockSpec((None, ml, w), lambda i, j: (i, 0, 0)),
                  pl.BlockSpec((None, ml, w), lambda i, j: (i, 0, 0))],
        out_specs=pl.BlockSpec((None, tq, w), lambda i, j: (i, j, 0)),
        out_shape=jax.ShapeDtypeStruct((b, t, w), out_dtype),
        compiler_params=_cparams("arbitrary", "arbitrary"),
        name="mem_attend",
    )(q3, gain.reshape(1, hd), k3, v3)


def _log_sigmoid(x):
    return -(jnp.maximum(-x, 0.0) + jnp.log1p(jnp.exp(-jnp.abs(x))))


def _logf_body(x_ref, w_ref, b_ref, o_ref, opad_ref):
    z = _dot(x_ref[...].astype(BF16), w_ref[...].astype(BF16))
    lf = _log_sigmoid(z + b_ref[...])
    opad_ref[...] = lf
    o_ref[...] = lf[:, :o_ref.shape[1]]


def _logf(x, w_pad, b_pad, h, tm=1024):
    m, k = x.shape
    tm = _tile(m, tm)
    return pl.pallas_call(
        _logf_body,
        grid=(m // tm,),
        in_specs=[pl.BlockSpec((tm, k), lambda i: (i, 0)),
                  pl.BlockSpec((k, LANE), lambda i: (0, 0)),
                  pl.BlockSpec((1, LANE), lambda i: (0, 0))],
        out_specs=[pl.BlockSpec((tm, h), lambda i: (i, 0)), pl.BlockSpec((tm, LANE), lambda i: (i, 0))],
        out_shape=[jax.ShapeDtypeStruct((m, h), F32), jax.ShapeDtypeStruct((m, LANE), F32)],
        compiler_params=_cparams("arbitrary"),
        name="logf",
    )(x, w_pad, b_pad)


def _cumsum_body(x_ref, o_ref):
    x = x_ref[...]
    t = x.shape[0]
    row = lax.broadcasted_iota(jnp.int32, x.shape, 0)
    s = 1
    while s < t:
        x = x + jnp.where(row >= s, pltpu.roll(x, s, 0), 0.0)
        s *= 2
    o_ref[...] = x


def _cumsum_time(x3):
    b, t, w = x3.shape
    return pl.pallas_call(
        _cumsum_body,
        grid=(b,),
        in_specs=[pl.BlockSpec((None, t, w), lambda i: (i, 0, 0))],
        out_specs=pl.BlockSpec((None, t, w), lambda i: (i, 0, 0)),
        out_shape=jax.ShapeDtypeStruct((b, t, w), F32),
        compiler_params=_cparams("arbitrary"),
        name="cumsum_time",
    )(x3)


def _fox_prompt_body(q_ref, k_ref, vt_ref, cq_ref, ck_ref, o_ref, ckb_ref, *, group, hd, tk):
    qi = pl.program_id(2)
    tq = q_ref.shape[0]
    cols = group * tq
    scale = hd ** -0.5
    q3 = jnp.concatenate([q_ref[:, g * hd:(g + 1) * hd] for g in range(group)], axis=0)
    cq = jnp.concatenate([cq_ref[g:g + 1, :] for g in range(group)], axis=1)

    @pl.when(qi == 0)
    def _():
        for g in range(group):
            ckb_ref[g] = jnp.broadcast_to(ck_ref[:, g:g + 1], ckb_ref.shape[1:])

    qpos = qi * tq + jnp.concatenate([lax.broadcasted_iota(jnp.int32, (tk, tq), 1)] * group, axis=1)
    krow = lax.broadcasted_iota(jnp.int32, (tk, cols), 0)

    def qk(ki):
        off = pl.multiple_of(ki * tk, tk)
        return _dot_nt(k_ref[pl.ds(off, tk), :].astype(BF16), q3)

    def softmax(ki, qk_raw, m, l, masked):
        off = pl.multiple_of(ki * tk, tk)
        ckb = jnp.concatenate([ckb_ref[g, pl.ds(off, tk), :] for g in range(group)], axis=1)
        s = qk_raw * scale + (cq - ckb)
        if masked:
            s = jnp.where(off + krow <= qpos, s, NEG_INF)
        m_new = jnp.maximum(m, jnp.max(s, axis=0, keepdims=True))
        alpha = jnp.exp(m - m_new)
        p = jnp.exp(s - m_new)
        l = alpha * l + jnp.sum(p, axis=0, keepdims=True)
        return m_new, l, alpha, p.astype(BF16)

    def pv(ki, p, alpha, acc):
        off = pl.multiple_of(ki * tk, tk)
        return alpha * acc + _dot(vt_ref[:, pl.ds(off, tk)].astype(BF16), p)

    def body(ki, carry):
        m, l, acc, qk_raw, p_prev, alpha_prev = carry
        qk_next = qk(ki + 1)
        acc = pv(jnp.maximum(ki - 1, 0), p_prev, alpha_prev, acc)
        m, l, alpha, p = softmax(ki, qk_raw, m, l, False)
        return m, l, acc, qk_next, p, alpha

    init = (jnp.full((1, cols), NEG_INF, F32), jnp.zeros((1, cols), F32), jnp.zeros((hd, cols), F32),
            qk(0), jnp.zeros((tk, cols), BF16), jnp.ones((1, cols), F32))
    n_full = (qi * tq) // tk
    m, l, acc, qk_raw, p_prev, alpha_prev = lax.fori_loop(0, n_full, body, init)
    acc = pv(jnp.maximum(n_full - 1, 0), p_prev, alpha_prev, acc)
    m, l, alpha, p = softmax(n_full, qk_raw, m, l, True)
    acc = pv(n_full, p, alpha, acc)
    out_t = acc / l
    for g in range(group):
        o_ref[:, g * hd:(g + 1) * hd] = out_t[:, g * tq:(g + 1) * tq].T.astype(o_ref.dtype)


def _fox_prompt(qn, k2, vt, c_col, c_row, batch, seq, kvh, group, hd, tk=256):
    m = qn.shape[0]
    tq = _tile(seq, LANE)
    tk = _tile(seq, tk)
    assert tk % tq == 0
    nq = seq // tq
    gw = group * hd
    return pl.pallas_call(
        functools.partial(_fox_prompt_body, group=group, hd=hd, tk=tk),
        grid=(batch, kvh, nq),
        in_specs=[pl.BlockSpec((tq, gw), lambda b, h, i: (b * nq + i, h)),
                  pl.BlockSpec((seq, hd), lambda b, h, i: (b, h)),
                  pl.BlockSpec((None, None, hd, seq), lambda b, h, i: (b, h, 0, 0)),
                  pl.BlockSpec((None, None, group, tq), lambda b, h, i: (b, h, 0, i)),
                  pl.BlockSpec((None, None, seq, group), lambda b, h, i: (b, h, 0, 0))],
        out_specs=pl.BlockSpec((tq, gw), lambda b, h, i: (b * nq + i, h)),
        out_shape=jax.ShapeDtypeStruct((m, kvh * gw), BF16),
        scratch_shapes=[pltpu.VMEM((group, seq, tq), F32)],
        compiler_params=_cparams("arbitrary", "arbitrary", "arbitrary"),
        name="fox_prompt",
    )(qn, k2, vt, c_row, c_col)


def _fox_decode_body(pt_ref, q_ref, kn_ref, vn_ref, lfn_ref, *refs, kvh, group, hd, pages):
    k_refs, v_refs, lf_refs = refs[:pages], refs[pages:2 * pages], refs[2 * pages:3 * pages]
    o_ref, m_ref, l_ref, acc_ref, d_ref = refs[3 * pages:]
    p = pl.program_id(1)
    heads = kvh * group
    ps = lf_refs[0].shape[1]
    scale = hd ** -0.5
    head_kv = lax.broadcasted_iota(jnp.int32, (heads, hd), 0) // group
    q = q_ref[...]

    def per_head(row_ref):
        out = jnp.zeros((heads, hd), F32)
        for j in range(kvh):
            out = out + jnp.where(head_kv == j, row_ref[:, j * hd:(j + 1) * hd], 0.0)
        return out

    @pl.when(p == 0)
    def _():
        m_ref[...] = jnp.sum(q.astype(F32) * per_head(kn_ref), axis=1, keepdims=True) * scale
        l_ref[...] = jnp.ones_like(l_ref)
        acc_ref[...] = per_head(vn_ref)
        d_ref[...] = lfn_ref[...]

    col_kv = lax.broadcasted_iota(jnp.int32, (heads, kvh * ps), 1) // ps
    own = col_kv == lax.broadcasted_iota(jnp.int32, (heads, kvh * ps), 0) // group
    lane = lax.broadcasted_iota(jnp.int32, (heads, ps), 1)
    qb = q.astype(BF16)
    d = d_ref[...]
    scores, values = [], []
    for k_ref, v_ref, lf_ref in zip(k_refs, v_refs, lf_refs):
        lf = lf_ref[...]
        suf = lf
        s = 1
        while s < ps:
            suf = suf + jnp.where(lane < ps - s, pltpu.roll(suf, ps - s, 1), 0.0)
            s *= 2
        bias = d + (suf - lf)
        d = d + suf[:, 0:1]
        kcat = jnp.concatenate([k_ref[pl.ds(j, ps, stride=kvh), :] for j in range(kvh)], axis=0).astype(BF16)
        values.append(jnp.concatenate([v_ref[pl.ds(j, ps, stride=kvh), :] for j in range(kvh)],
                                      axis=0).astype(BF16))
        sc = _dot_nt(qb, kcat) * scale + jnp.concatenate([bias] * kvh, axis=1)
        scores.append(jnp.where(own, sc, NEG_INF))
    d_ref[...] = d
    m_old = m_ref[...]
    m_new = m_old
    for sc in scores:
        m_new = jnp.maximum(m_new, jnp.max(sc, axis=-1, keepdims=True))
    alpha = jnp.exp(m_old - m_new)
    l_new = alpha * l_ref[...]
    acc = alpha * acc_ref[...]
    for sc, vcat in zip(scores, values):
        pr = jnp.exp(sc - m_new)
        l_new = l_new + jnp.sum(pr, axis=-1, keepdims=True)
        acc = acc + _dot(pr.astype(BF16), vcat)
    l_ref[...] = l_new
    acc_ref[...] = acc
    m_ref[...] = m_new

    @pl.when(p == pl.num_programs(1) - 1)
    def _():
        o_ref[...] = acc_ref[...] / l_ref[...]


def _fox_decode(page_table, q3, k_new, v_new, lf_new, cache_k, cache_v, cache_lf_t, kvh, group, hd, pages=8):
    db, heads, _ = q3.shape
    npg = page_table.shape[1]
    ps = cache_lf_t.shape[2]
    w = kvh * hd
    while npg % pages:
        pages -= 1

    def page3(i):
        return lambda b, p, pt: (pt[b, npg - 1 - (p * pages + i)], 0, 0)

    grid_spec = pltpu.PrefetchScalarGridSpec(
        num_scalar_prefetch=1,
        grid=(db, npg // pages),
        in_specs=[pl.BlockSpec((None, heads, hd), lambda b, p, pt: (b, 0, 0)),
                  pl.BlockSpec((None, 1, w), lambda b, p, pt: (b, 0, 0)),
                  pl.BlockSpec((None, 1, w), lambda b, p, pt: (b, 0, 0)),
                  pl.BlockSpec((None, heads, 1), lambda b, p, pt: (b, 0, 0))]
                 + [pl.BlockSpec((None, ps * kvh, hd), page3(i)) for i in range(pages)]
                 + [pl.BlockSpec((None, ps * kvh, hd), page3(i)) for i in range(pages)]
                 + [pl.BlockSpec((None, heads, ps), page3(i)) for i in range(pages)],
        out_specs=pl.BlockSpec((None, heads, hd), lambda b, p, pt: (b, 0, 0)),
        scratch_shapes=[pltpu.VMEM((heads, 1), F32), pltpu.VMEM((heads, 1), F32),
                        pltpu.VMEM((heads, hd), F32), pltpu.VMEM((heads, 1), F32)],
    )
    return pl.pallas_call(
        functools.partial(_fox_decode_body, kvh=kvh, group=group, hd=hd, pages=pages),
        grid_spec=grid_spec,
        out_shape=jax.ShapeDtypeStruct((db, heads, hd), F32),
        compiler_params=_cparams("arbitrary", "arbitrary"),
        name="fox_decode",
    )(page_table, q3, k_new, v_new, lf_new, *([cache_k] * pages), *([cache_v] * pages), *([cache_lf_t] * pages))


def _router_body(x_ref, w_ref, b_ref, idx_ref, gate_ref, *, n_exp):
    logits = jnp.dot(x_ref[...], w_ref[...], preferred_element_type=F32,
                     precision=lax.Precision.HIGHEST) + b_ref[...]
    lane = lax.broadcasted_iota(jnp.int32, logits.shape, 1)
    logits = jnp.where(lane < n_exp, logits, NEG_INF)
    m1 = jnp.max(logits, axis=-1, keepdims=True)
    i1 = jnp.min(jnp.where(logits == m1, lane, LANE), axis=-1, keepdims=True)
    rest = jnp.where(lane == i1, NEG_INF, logits)
    m2 = jnp.max(rest, axis=-1, keepdims=True)
    i2 = jnp.min(jnp.where(rest == m2, lane, LANE), axis=-1, keepdims=True)
    e = jnp.exp(m2 - m1)
    g1 = 1.0 / (1.0 + e)
    g2 = e / (1.0 + e)
    idx_ref[...] = jnp.where(lane == 0, i1, jnp.where(lane == 1, i2, 0))
    gate_ref[...] = jnp.where(lane == 0, g1, jnp.where(lane == 1, g2, 0.0))


def _router(x, w_pad, b_pad, n_exp):
    m, k = x.shape
    tm = _tile(m, 256)
    return pl.pallas_call(
        functools.partial(_router_body, n_exp=n_exp),
        grid=(m // tm,),
        in_specs=[pl.BlockSpec((tm, k), lambda i: (i, 0)),
                  pl.BlockSpec((k, LANE), lambda i: (0, 0)),
                  pl.BlockSpec((1, LANE), lambda i: (0, 0))],
        out_specs=[pl.BlockSpec((tm, LANE), lambda i: (i, 0)), pl.BlockSpec((tm, LANE), lambda i: (i, 0))],
        out_shape=[jax.ShapeDtypeStruct((m, LANE), jnp.int32), jax.ShapeDtypeStruct((m, LANE), F32)],
        compiler_params=_cparams("arbitrary"),
        name="router",
    )(x, w_pad, b_pad)


def _gather_body(tok_ref, used_ref, a_ref, b_ref, o_ref, buf_ref, sem, *, tg):
    i = pl.program_id(0)
    ma = a_ref.shape[0]
    slot = i % 2

    def row_copy(src_ref, t, r, s):
        return pltpu.make_async_copy(src_ref.at[pl.ds(t, 1)], buf_ref.at[s, pl.ds(r, 1)], sem.at[s])

    def issue_tile(tile, s):
        base = tile * tg

        @pl.when(base < used_ref[0])
        def _():
            def issue(r8, c):
                for u in range(8):
                    r = r8 * 8 + u
                    t = tok_ref[base + r]

                    @pl.when(t < ma)
                    def _():
                        row_copy(a_ref, t, r, s).start(priority=u % 2)

                    @pl.when(t >= ma)
                    def _():
                        row_copy(b_ref, t - ma, r, s).start(priority=u % 2)
                return c

            lax.fori_loop(0, tg // 8, issue, 0)

    @pl.when(i == 0)
    def _():
        issue_tile(0, 0)

    @pl.when(i + 1 < pl.num_programs(0))
    def _():
        issue_tile(i + 1, 1 - slot)

    @pl.when(i * tg < used_ref[0])
    def _():
        def drain(r, c):
            row_copy(a_ref, 0, r, slot).wait()
            return c

        lax.fori_loop(0, tg, drain, 0, unroll=8)
        o_ref[...] = buf_ref[slot].astype(o_ref.dtype)

    @pl.when(i * tg >= used_ref[0])
    def _():
        o_ref[...] = jnp.zeros_like(o_ref)


def _gather_rows(row_token, used_rows, src_a, src_b, tg=256):
    p = row_token.shape[0]
    d = src_a.shape[1]
    tg = _tile(p, tg)
    grid_spec = pltpu.PrefetchScalarGridSpec(
        num_scalar_prefetch=2,
        grid=(p // tg,),
        in_specs=[pl.BlockSpec(memory_space=pl.ANY), pl.BlockSpec(memory_space=pl.ANY)],
        out_specs=pl.BlockSpec((tg, d), lambda i, tok, used: (i, 0)),
        scratch_shapes=[pltpu.VMEM((2, tg, d), F32), pltpu.SemaphoreType.DMA((2,))],
    )
    return pl.pallas_call(
        functools.partial(_gather_body, tg=tg),
        grid_spec=grid_spec,
        out_shape=jax.ShapeDtypeStruct((p, d), BF16),
        compiler_params=_cparams("arbitrary"),
        name="moe_gather",
    )(row_token, used_rows, src_a, src_b)


def _combine_body(pos_ref, y_ref, h_ref, g_ref, o_ref, buf_ref, sem, *, tt):
    i = pl.program_id(0)
    base = i * tt

    def row_copy(k, src_row, r):
        return pltpu.make_async_copy(y_ref.at[pl.ds(src_row, 1)], buf_ref.at[k, pl.ds(r, 1)], sem)

    def issue(r, c):
        row_copy(0, pos_ref[2 * (base + r)], r).start(priority=0)
        row_copy(1, pos_ref[2 * (base + r) + 1], r).start(priority=1)
        return c

    lax.fori_loop(0, tt, issue, 0)

    def drain(r, c):
        row_copy(0, 0, r).wait()
        row_copy(1, 0, r).wait()
        return c

    lax.fori_loop(0, tt, drain, 0)
    g = g_ref[...]
    o_ref[...] = h_ref[...] + (g[:, 0:1] * buf_ref[0] + g[:, 1:2] * buf_ref[1])


def _combine(pos_flat, y_sorted, h, gates, tt=256):
    m, d = h.shape
    tt = _tile(m, tt)
    grid_spec = pltpu.PrefetchScalarGridSpec(
        num_scalar_prefetch=1,
        grid=(m // tt,),
        in_specs=[pl.BlockSpec(memory_space=pl.ANY),
                  pl.BlockSpec((tt, d), lambda i, pos: (i, 0)),
                  pl.BlockSpec((tt, LANE), lambda i, pos: (i, 0))],
        out_specs=pl.BlockSpec((tt, d), lambda i, pos: (i, 0)),
        scratch_shapes=[pltpu.VMEM((2, tt, d), F32), pltpu.SemaphoreType.DMA(())],
    )
    return pl.pallas_call(
        functools.partial(_combine_body, tt=tt),
        grid_spec=grid_spec,
        out_shape=jax.ShapeDtypeStruct((m, d), F32),
        compiler_params=_cparams("arbitrary"),
        name="moe_combine",
    )(pos_flat, y_sorted, h, gates)


def _for_row_blocks(nrows, total, big, small, live, dead=None):
    def run(fn, size, lo, hi):
        def body(t, carry):
            fn(pl.ds(pl.multiple_of(t * size, size), size), size)
            return carry
        lax.fori_loop(lo, hi, body, 0)

    nbig = nrows // big
    run(live, big, 0, nbig)
    done = nbig * (big // small)
    end = done + (nrows - nbig * big + small - 1) // small
    run(live, small, done, end)
    if dead is not None:
        run(dead, small, end, total // small)


def _moe_glu_body(ce_ref, cx_ref, cv_ref, used_ref, x_ref, wg_ref, wu_ref, o_ref, *, big, small):
    c = pl.program_id(0)

    @pl.when(c < used_ref[0])
    def _():
        def live(rows, size):
            x = x_ref[rows, :]
            g = _dot(x, wg_ref[...].astype(BF16))
            u = _dot(x, wu_ref[...].astype(BF16))
            o_ref[rows, :] = _silu_mul(g, u).astype(o_ref.dtype)

        def dead(rows, size):
            o_ref[rows, :] = jnp.zeros((size, o_ref.shape[1]), o_ref.dtype)

        _for_row_blocks(cv_ref[c], x_ref.shape[0], big, small, live, dead)

    @pl.when(c >= used_ref[0])
    def _():
        o_ref[...] = jnp.zeros_like(o_ref)


def _moe_glu(sched, x_sorted, we_gu, chunk, big, small, tn=256):
    ce, cx, cv, used = sched
    p, d = x_sorted.shape
    f = we_gu.shape[2] // 2
    tn = _tile(f, tn)
    nj = f // tn
    nc = p // chunk

    def jpin(c, j, used):
        return jnp.where(c < used[0], j, nj - 1)

    grid_spec = pltpu.PrefetchScalarGridSpec(
        num_scalar_prefetch=4,
        grid=(nc, nj),
        in_specs=[pl.BlockSpec((chunk, d), lambda c, j, ce, cx, cv, used: (cx[c], 0),
                               pipeline_mode=pl.Buffered(1)),
                  pl.BlockSpec((None, d, tn), lambda c, j, ce, cx, cv, used: (ce[c], 0, jpin(c, j, used))),
                  pl.BlockSpec((None, d, tn), lambda c, j, ce, cx, cv, used: (ce[c], 0, nj + jpin(c, j, used)))],
        out_specs=pl.BlockSpec((chunk, tn), lambda c, j, ce, cx, cv, used: (c, j)),
    )
    return pl.pallas_call(
        functools.partial(_moe_glu_body, big=big, small=small),
        grid_spec=grid_spec,
        out_shape=jax.ShapeDtypeStruct((p, f), BF16),
        compiler_params=_cparams("arbitrary", "arbitrary"),
        name="moe_glu",
    )(ce, cx, cv, used, x_sorted, we_gu, we_gu)


def _moe_down_body(ce_ref, cx_ref, cv_ref, used_ref, x_ref, w_ref, o_ref, *, big, small):
    c, kk = pl.program_id(0), pl.program_id(2)

    @pl.when(c < used_ref[0])
    def _():
        def first(rows, size):
            o_ref[rows, :] = _dot(x_ref[rows, :], w_ref[...].astype(BF16))

        def accumulate(rows, size):
            o_ref[rows, :] += _dot(x_ref[rows, :], w_ref[...].astype(BF16))

        def dead(rows, size):
            o_ref[rows, :] = jnp.zeros((size, o_ref.shape[1]), o_ref.dtype)

        @pl.when(kk == 0)
        def _():
            _for_row_blocks(cv_ref[c], x_ref.shape[0], big, small, first, dead)

        @pl.when(kk > 0)
        def _():
            _for_row_blocks(cv_ref[c], x_ref.shape[0], big, small, accumulate)

    @pl.when((c >= used_ref[0]) & (kk == 0))
    def _():
        o_ref[...] = jnp.zeros_like(o_ref)


def _moe_down(sched, act_sorted, we_down, chunk, big, small, tn=512, tk=2048):
    ce, cx, cv, used = sched
    p, f = act_sorted.shape
    d = we_down.shape[2]
    tn, tk = _tile(d, tn), _tile(f, tk)
    nj, nk = d // tn, f // tk
    nc = p // chunk

    def pin(c, v, last, used):
        return jnp.where(c < used[0], v, last)

    grid_spec = pltpu.PrefetchScalarGridSpec(
        num_scalar_prefetch=4,
        grid=(nc, nj, nk),
        in_specs=[pl.BlockSpec((chunk, tk), lambda c, j, k, ce, cx, cv, used: (cx[c], pin(c, k, nk - 1, used))),
                  pl.BlockSpec((None, tk, tn), lambda c, j, k, ce, cx, cv, used:
                               (ce[c], pin(c, k, nk - 1, used), pin(c, j, nj - 1, used)))],
        out_specs=pl.BlockSpec((chunk, tn), lambda c, j, k, ce, cx, cv, used: (c, j)),
    )
    return pl.pallas_call(
        functools.partial(_moe_down_body, big=big, small=small),
        grid_spec=grid_spec,
        out_shape=jax.ShapeDtypeStruct((p, d), F32),
        compiler_params=_cparams("arbitrary", "arbitrary", "arbitrary"),
        name="moe_down",
    )(ce, cx, cv, used, act_sorted, we_down)


def _moe(h_p, h_s, hn_p, hn_s, w_router, b_router, we_gu, we_down):
    mp, d = hn_p.shape
    ms = hn_s.shape[0]
    n_exp = we_gu.shape[0]
    mt = mp + ms
    w_pad = jnp.pad(w_router, ((0, 0), (0, LANE - n_exp)))
    b_pad = jnp.pad(b_router.reshape(1, n_exp).astype(F32), ((0, 0), (0, LANE - n_exp)))
    idx_p, gate_p = _router(hn_p, w_pad, b_pad, n_exp)
    idx_s, gate_s = _router(hn_s, w_pad, b_pad, n_exp)

    chunk = min(MOE_CHUNK, -(-2 * mt // MOE_SMALL) * MOE_SMALL)
    big = min(MOE_BIG, chunk)
    idx = jnp.concatenate([idx_p[:, :2], idx_s[:, :2]], axis=0)
    sel = (idx[:, :, None] == jnp.arange(n_exp, dtype=jnp.int32)[None, None, :]).any(axis=1).astype(jnp.int32)
    rank = jnp.cumsum(sel, axis=0) - sel
    cnt = jnp.sum(sel, axis=0)
    nch = (cnt + chunk - 1) // chunk
    ch_end = jnp.cumsum(nch)
    ch_off = ch_end - nch
    n_chunks = (2 * mt + n_exp * (chunk - 1)) // chunk
    p_rows = n_chunks * chunk
    pos = jnp.take_along_axis(ch_off[None, :] * chunk + rank, idx, axis=1)
    tok = jnp.repeat(jnp.arange(mt, dtype=jnp.int32), 2)
    row_token = jnp.zeros((p_rows,), jnp.int32).at[pos.reshape(-1)].set(tok)
    used = ch_end[-1].astype(jnp.int32)
    cid = jnp.minimum(jnp.arange(n_chunks, dtype=jnp.int32), used - 1)
    ce = jnp.minimum(jnp.sum((ch_end[None, :] <= cid[:, None]).astype(jnp.int32), axis=1), n_exp - 1)
    cv = jnp.clip(cnt[ce] - (cid - ch_off[ce]) * chunk, 0, chunk).astype(jnp.int32)
    sched = (ce, cid, cv, used.reshape(1))

    x_sorted = _gather_rows(row_token, (used * chunk).reshape(1), hn_p, hn_s)
    act = _moe_glu(sched, x_sorted, we_gu, chunk, big, MOE_SMALL)
    y_sorted = _moe_down(sched, act, we_down, chunk, big, MOE_SMALL)
    pos = pos.astype(jnp.int32)
    out_p = _combine(pos[:mp].reshape(-1), y_sorted, h_p, gate_p)
    out_s = _combine(pos[mp:].reshape(-1), y_sorted, h_s, gate_s)
    return out_p, out_s


def kernel(x_prompt, x_sample, mem_prompt, cache_k, cache_v, cache_logf, state_conv, cache_mem_k, cache_mem_v, page_table, norm_mix, norm_ffn, w_in_a, conv_w, w_in_b, q_norm_b, w_out, w_mem_kv, mem_q_norm, mem_k_norm, kv_norm, w_kv, b_f, k_norm, w_gu, w_down, w_router, b_router, we_gu, we_down):
    bsz, seq, d = x_prompt.shape
    db, dec_seq, _ = x_sample.shape
    assert dec_seq == 1, "the decode path handles one new token per sequence"
    depth, _, ml, mem_heads, mem_hd = cache_mem_k.shape
    assert depth == 2 and w_in_a.shape[0] == 1 and w_in_b.shape[0] == 1
    mem_w = mem_heads * mem_hd
    n_pool, ps, kvh, hd = cache_k.shape
    heads = cache_logf.shape[2]
    group = heads // kvh
    kv_w = kvh * hd
    cd = conv_w.shape[2]
    assert conv_w.shape[1] == 3 and state_conv.shape[2] == 2
    mp = bsz * seq

    mem_bf = mem_prompt.reshape(bsz * ml, d).astype(BF16)
    mk, mv = [], []
    for i in range(depth):
        mk.append(_mm_headnorm(mem_bf, w_mem_kv, i, 0, mem_w, mem_k_norm[i], mem_hd, F32))
        mv.append(_mm(mem_bf, w_mem_kv, i, mem_w, mem_w, F32))
    mem_k_prompt = jnp.stack(mk).reshape(depth, bsz, ml, mem_heads, mem_hd)
    mem_v_prompt = jnp.stack(mv).reshape(depth, bsz, ml, mem_heads, mem_hd)

    w_kv3 = w_kv.reshape(1, d, -1)
    w_lf = jnp.pad(w_kv[:, 2 * kv_w:], ((0, 0), (0, LANE - heads)))
    b_lf = jnp.pad(b_f.reshape(1, heads).astype(F32), ((0, 0), (0, LANE - heads)))
    we_gu3 = we_gu.reshape(we_gu.shape[1:])
    we_down3 = we_down.reshape(we_down.shape[1:])

    def layer0_tail(h0, main, q_mem, mem_k0, mem_v0, nb, act_dtype):
        t = h0.shape[0] // nb
        mo = _mem_attend(q_mem.reshape(nb, t, mem_w), mem_q_norm[0], mem_k0.reshape(nb, ml, mem_w),
                         mem_v0.reshape(nb, ml, mem_w), mem_heads, act_dtype).reshape(nb * t, mem_w)
        h1 = _mm_out(main, mo, w_out, 0, h0)
        (hn,) = _rmsnorm(h1, norm_ffn[0], (act_dtype,))
        act = _mm_glu(hn, w_gu, 0, act_dtype)
        h2 = _mm_down(act, w_down, 0, h1)
        (kvn,) = _rmsnorm(h2, kv_norm, (act_dtype,))
        k = _mm_headnorm(kvn, w_kv3, 0, 0, kv_w, k_norm, hd, F32)
        v = _mm(kvn, w_kv3, 0, kv_w, kv_w, F32)
        logf, logf_pad = _logf(kvn, w_lf, b_lf, heads)
        (xn1,) = _rmsnorm(h2, norm_mix[1], (act_dtype,))
        qn = _mm_headnorm(xn1, w_in_b, 0, 0, heads * hd, q_norm_b[0], hd, act_dtype)
        q_mem1 = _mm(xn1, w_in_b, 0, heads * hd, mem_w, F32)
        return h2, k, v, logf, logf_pad, qn, q_mem1

    def layer1_attn_out(h2, attn, q_mem1, mem_k1, mem_v1, nb, act_dtype):
        t = h2.shape[0] // nb
        mo = _mem_attend(q_mem1.reshape(nb, t, mem_w), mem_q_norm[1], mem_k1.reshape(nb, ml, mem_w),
                         mem_v1.reshape(nb, ml, mem_w), mem_heads, act_dtype).reshape(nb * t, mem_w)
        h3 = _mm_out(attn, mo, w_out, 1, h2)
        (hn3,) = _rmsnorm(h3, norm_ffn[1], (F32,))
        return h3, hn3

    xp = x_prompt.reshape(mp, d)
    (xn,) = _rmsnorm(xp, norm_mix[0], (BF16,))
    main_p, conv_tail = _conv_prompt(xn, w_in_a, conv_w, 0, bsz, seq)
    q_mem_p = _mm(xn, w_in_a, 0, 3 * cd, mem_w, F32)
    h2_p, k_p, v_p, logf_p, logf_pad_p, qn_p, q_mem1_p = layer0_tail(
        xp, main_p, q_mem_p, mk[0], mv[0], bsz, BF16)
    c = _cumsum_time(logf_pad_p.reshape(bsz, seq, LANE))[:, :, :heads].reshape(bsz, seq, kvh, group)
    vt_p = v_p.reshape(bsz, seq, kvh, hd).transpose(0, 2, 3, 1)
    attn_p = _fox_prompt(qn_p, k_p, vt_p, c.transpose(0, 2, 1, 3), c.transpose(0, 2, 3, 1),
                         bsz, seq, kvh, group, hd)
    h3_p, hn3_p = layer1_attn_out(h2_p, attn_p, q_mem1_p, mk[1], mv[1], bsz, BF16)

    xs = x_sample.reshape(db, d)
    (xn_s,) = _rmsnorm(xs, norm_mix[0], (F32,))
    z_s = _mm(xn_s, w_in_a, 0, 0, 3 * cd + mem_w, F32)
    main_s, conv_new_s = _conv_step(z_s, state_conv[0], conv_w[0])
    h2_s, k_s, v_s, logf_s, _, qn_s, q_mem1_s = layer0_tail(
        xs, main_s, z_s[:, 3 * cd:], cache_mem_k[0], cache_mem_v[0], db, F32)
    attn_s = _fox_decode(page_table, qn_s.reshape(db, heads, hd), k_s.reshape(db, 1, kv_w),
                         v_s.reshape(db, 1, kv_w), logf_s.reshape(db, heads, 1),
                         cache_k.reshape(n_pool, ps * kvh, hd), cache_v.reshape(n_pool, ps * kvh, hd),
                         cache_logf.transpose(0, 2, 1), kvh, group, hd).reshape(db, heads * hd)
    h3_s, hn3_s = layer1_attn_out(h2_s, attn_s, q_mem1_s, cache_mem_k[1], cache_mem_v[1], db, F32)

    y_p, y_s = _moe(h3_p, h3_s, hn3_p, hn3_s, w_router[0], b_router[0], we_gu3, we_down3)

    return (y_p.reshape(bsz, seq, d), y_s.reshape(db, 1, d),
            k_p.reshape(bsz, seq, kvh, hd), v_p.reshape(bsz, seq, kvh, hd), logf_p.reshape(bsz, seq, heads),
            conv_tail.reshape(1, bsz, 2, cd), mem_k_prompt, mem_v_prompt,
            k_s.reshape(db, 1, kvh, hd), v_s.reshape(db, 1, kvh, hd), logf_s.reshape(db, 1, heads),
            conv_new_s.reshape(1, db, 2, cd))
```

```python
import functools

import jax
import jax.numpy as jnp
from jax import lax
from jax.experimental import pallas as pl
from jax.experimental.pallas import tpu as pltpu

F32 = jnp.float32
BF16 = jnp.bfloat16
EPS = 1e-6
NEG_INF = float("-inf")

V7X_VMEM_BYTES = 64 * 1024 * 1024
VMEM_LIMIT = V7X_VMEM_BYTES - 8 * 1024 * 1024
LANE = 128
MOE_CHUNK = 2560
MOE_BIG = 512
MOE_SMALL = 128


def _cparams(*sem):
    return pltpu.CompilerParams(dimension_semantics=sem, vmem_limit_bytes=VMEM_LIMIT)


def _tile(n, pref):
    if n <= pref:
        return n
    t = pref
    while n % t:
        t -= LANE
    assert t > 0
    return t


def _dot(a, b):
    return jnp.dot(a, b, preferred_element_type=F32)


def _dot_nt(a, b):
    return lax.dot_general(a, b, (((1,), (1,)), ((), ())), preferred_element_type=F32)


def _rms(x, g):
    return x * lax.rsqrt(jnp.mean(x * x, axis=-1, keepdims=True) + EPS) * g


def _rmsnorm_body(x_ref, g_ref, *o_refs):
    y = _rms(x_ref[...].astype(F32), g_ref[...])
    for o in o_refs:
        o[...] = y.astype(o.dtype)


def _rmsnorm(x, g, dtypes):
    m, d = x.shape
    tr = _tile(m, 256)
    outs = pl.pallas_call(
        _rmsnorm_body,
        grid=(m // tr,),
        in_specs=[pl.BlockSpec((tr, d), lambda i: (i, 0)), pl.BlockSpec((1, d), lambda i: (0, 0))],
        out_specs=[pl.BlockSpec((tr, d), lambda i: (i, 0)) for _ in dtypes],
        out_shape=[jax.ShapeDtypeStruct((m, d), dt) for dt in dtypes],
        compiler_params=_cparams("arbitrary"),
        name="rmsnorm",
    )(x, g.reshape(1, d))
    return outs


def _mm_body(x_ref, w_ref, o_ref):
    o_ref[...] = _dot(x_ref[...].astype(BF16), w_ref[...].astype(BF16)).astype(o_ref.dtype)


def _mm(x, w3, layer, c0, n, out_dtype, tm=1024, tn=512):
    m, k = x.shape
    tm, tn = _tile(m, tm), _tile(n, tn)
    assert c0 % tn == 0
    return pl.pallas_call(
        _mm_body,
        grid=(m // tm, n // tn),
        in_specs=[pl.BlockSpec((tm, k), lambda i, j: (i, 0)),
                  pl.BlockSpec((None, k, tn), lambda i, j: (layer, 0, c0 // tn + j))],
        out_specs=pl.BlockSpec((tm, tn), lambda i, j: (i, j)),
        out_shape=jax.ShapeDtypeStruct((m, n), out_dtype),
        compiler_params=_cparams("arbitrary", "arbitrary"),
        name="mm",
    )(x, w3)


def _mm_headnorm_body(x_ref, w_ref, g_ref, o_ref, *, hd):
    z = _dot(x_ref[...].astype(BF16), w_ref[...].astype(BF16))
    for h in range(z.shape[1] // hd):
        o_ref[:, h * hd:(h + 1) * hd] = _rms(z[:, h * hd:(h + 1) * hd], g_ref[...]).astype(o_ref.dtype)


def _mm_headnorm(x, w3, layer, c0, n, gain, hd, out_dtype, tm=1024, tn=512):
    m, k = x.shape
    tm, tn = _tile(m, tm), _tile(n, tn)
    assert c0 % tn == 0 and tn % hd == 0
    return pl.pallas_call(
        functools.partial(_mm_headnorm_body, hd=hd),
        grid=(m // tm, n // tn),
        in_specs=[pl.BlockSpec((tm, k), lambda i, j: (i, 0)),
                  pl.BlockSpec((None, k, tn), lambda i, j: (layer, 0, c0 // tn + j)),
                  pl.BlockSpec((1, hd), lambda i, j: (0, 0))],
        out_specs=pl.BlockSpec((tm, tn), lambda i, j: (i, j)),
        out_shape=jax.ShapeDtypeStruct((m, n), out_dtype),
        compiler_params=_cparams("arbitrary", "arbitrary"),
        name="mm_headnorm",
    )(x, w3, gain.reshape(1, hd))


def _mm_out_body(x1_ref, x2_ref, w_ref, r_ref, o_ref):
    k1 = x1_ref.shape[1]
    w = w_ref[...].astype(BF16)
    acc = _dot(x1_ref[...].astype(BF16), w[:k1]) + _dot(x2_ref[...].astype(BF16), w[k1:])
    o_ref[...] = r_ref[...] + acc


def _mm_out(x1, x2, w3, layer, resid, tm=1024, tn=512):
    m, k1 = x1.shape
    k2 = x2.shape[1]
    n = w3.shape[2]
    tm, tn = _tile(m, tm), _tile(n, tn)
    return pl.pallas_call(
        _mm_out_body,
        grid=(m // tm, n // tn),
        in_specs=[pl.BlockSpec((tm, k1), lambda i, j: (i, 0)),
                  pl.BlockSpec((tm, k2), lambda i, j: (i, 0)),
                  pl.BlockSpec((None, k1 + k2, tn), lambda i, j: (layer, 0, j)),
                  pl.BlockSpec((tm, tn), lambda i, j: (i, j))],
        out_specs=pl.BlockSpec((tm, tn), lambda i, j: (i, j)),
        out_shape=jax.ShapeDtypeStruct((m, n), F32),
        compiler_params=_cparams("arbitrary", "arbitrary"),
        name="mm_out",
    )(x1, x2, w3, resid)


def _silu_mul(g, u):
    return g * jax.nn.sigmoid(g) * u


def _mm_glu_body(x_ref, wg_ref, wu_ref, o_ref):
    x = x_ref[...].astype(BF16)
    g = _dot(x, wg_ref[...].astype(BF16))
    u = _dot(x, wu_ref[...].astype(BF16))
    o_ref[...] = _silu_mul(g, u).astype(o_ref.dtype)


def _mm_glu(x, w3, layer, out_dtype, tm=1024, tn=256):
    m, k = x.shape
    f = w3.shape[2] // 2
    tm, tn = _tile(m, tm), _tile(f, tn)
    return pl.pallas_call(
        _mm_glu_body,
        grid=(m // tm, f // tn),
        in_specs=[pl.BlockSpec((tm, k), lambda i, j: (i, 0)),
                  pl.BlockSpec((None, k, tn), lambda i, j: (layer, 0, j)),
                  pl.BlockSpec((None, k, tn), lambda i, j: (layer, 0, f // tn + j))],
        out_specs=pl.BlockSpec((tm, tn), lambda i, j: (i, j)),
        out_shape=jax.ShapeDtypeStruct((m, f), out_dtype),
        compiler_params=_cparams("arbitrary", "arbitrary"),
        name="mm_glu",
    )(x, w3, w3)


def _mm_down_body(x_ref, w_ref, r_ref, o_ref):
    kk = pl.program_id(2)
    part = _dot(x_ref[...].astype(BF16), w_ref[...].astype(BF16))

    @pl.when(kk == 0)
    def _():
        o_ref[...] = part

    @pl.when(kk > 0)
    def _():
        o_ref[...] += part

    @pl.when(kk == pl.num_programs(2) - 1)
    def _():
        o_ref[...] = r_ref[...] + o_ref[...]


def _mm_down(x, w3, layer, resid, tm=2048, tn=512, tk=2048):
    m, k = x.shape
    n = w3.shape[2]
    tm, tn, tk = _tile(m, tm), _tile(n, tn), _tile(k, tk)
    return pl.pallas_call(
        _mm_down_body,
        grid=(m // tm, n // tn, k // tk),
        in_specs=[pl.BlockSpec((tm, tk), lambda i, j, kk: (i, kk)),
                  pl.BlockSpec((None, tk, tn), lambda i, j, kk: (layer, kk, j)),
                  pl.BlockSpec((tm, tn), lambda i, j, kk: (i, j))],
        out_specs=pl.BlockSpec((tm, tn), lambda i, j, kk: (i, j)),
        out_shape=jax.ShapeDtypeStruct((m, n), F32),
        compiler_params=_cparams("arbitrary", "arbitrary", "arbitrary"),
        name="mm_down",
    )(x, w3, resid)


def _conv_prompt_body(x_ref, wb_ref, wc_ref, wh_ref, cw_ref, y_ref, tail_ref, carry_ref, *, tiles_per_seq):
    i, j = pl.program_id(0), pl.program_id(1)
    x = x_ref[...]
    b = _dot(x, wb_ref[...].astype(BF16))
    c = _dot(x, wc_ref[...].astype(BF16))
    hv = _dot(x, wh_ref[...].astype(BF16))
    u = c * hv
    tm = u.shape[0]
    @pl.when(i % tiles_per_seq == 0)
    def _():
        carry_ref[j] = jnp.zeros(carry_ref.shape[1:], F32)

    prev = carry_ref[j]
    row = lax.broadcasted_iota(jnp.int32, u.shape, 0)
    u1 = jnp.where(row == 0, prev[1:2], pltpu.roll(u, 1, 0))
    u2 = jnp.where(row == 0, prev[0:1], jnp.where(row == 1, prev[1:2], pltpu.roll(u, 2, 0)))
    cw = cw_ref[...]
    y = cw[0:1] * u2 + cw[1:2] * u1 + cw[2:3] * u
    y_ref[...] = (b * y).astype(y_ref.dtype)
    carry_ref[j] = u[tm - 2:tm]
    tail_ref[...] = u[tm - 2:tm]


def _conv_prompt(xn, w_in_a, conv_w, layer, batch, seq, tm=1024, tn=256):
    m, d = xn.shape
    cd = conv_w.shape[2]
    tm, tn = _tile(seq, tm), _tile(cd, tn)
    nj = cd // tn
    tps = seq // tm
    y, tail = pl.pallas_call(
        functools.partial(_conv_prompt_body, tiles_per_seq=tps),
        grid=(m // tm, nj),
        in_specs=[pl.BlockSpec((tm, d), lambda i, j: (i, 0)),
                  pl.BlockSpec((None, d, tn), lambda i, j: (layer, 0, j)),
                  pl.BlockSpec((None, d, tn), lambda i, j: (layer, 0, nj + j)),
                  pl.BlockSpec((None, d, tn), lambda i, j: (layer, 0, 2 * nj + j)),
                  pl.BlockSpec((None, 3, tn), lambda i, j: (layer, 0, j))],
        out_specs=[pl.BlockSpec((tm, tn), lambda i, j: (i, j)),
                   pl.BlockSpec((None, 2, tn), lambda i, j: (i, 0, j))],
        out_shape=[jax.ShapeDtypeStruct((m, cd), BF16), jax.ShapeDtypeStruct((m // tm, 2, cd), F32)],
        scratch_shapes=[pltpu.VMEM((nj, 2, tn), F32)],
        compiler_params=_cparams("arbitrary", "arbitrary"),
        name="conv_prompt",
    )(xn, w_in_a, w_in_a, w_in_a, conv_w)
    return y, tail[tps - 1::tps]


def _conv_step_body(z_ref, s_ref, cw_ref, y_ref, ns_ref, *, cd):
    b, c, hv = z_ref[:, 0:cd], z_ref[:, cd:2 * cd], z_ref[:, 2 * cd:3 * cd]
    u = c * hv
    s0, s1 = s_ref[:, 0, :], s_ref[:, 1, :]
    cw = cw_ref[...]
    y = cw[0:1] * s0 + cw[1:2] * s1 + cw[2:3] * u
    y_ref[...] = b * y
    ns_ref[:, 0, :] = s1
    ns_ref[:, 1, :] = u


def _conv_step(z, state, cw):
    db, _, cd = state.shape
    return pl.pallas_call(
        functools.partial(_conv_step_body, cd=cd),
        out_shape=[jax.ShapeDtypeStruct((db, cd), F32), jax.ShapeDtypeStruct((db, 2, cd), F32)],
        compiler_params=pltpu.CompilerParams(vmem_limit_bytes=VMEM_LIMIT),
        name="conv_step",
    )(z, state, cw)


def _mem_attend_body(q_ref, g_ref, k_ref, v_ref, o_ref, *, heads):
    hd = q_ref.shape[-1] // heads
    scale = hd ** -0.5
    for h in range(heads):
        sl = slice(h * hd, (h + 1) * hd)
        q = _rms(q_ref[:, sl].astype(F32), g_ref[...]).astype(BF16)
        s = _dot_nt(q, k_ref[:, sl].astype(BF16)) * scale
        s = s - jnp.max(s, axis=-1, keepdims=True)
        p = jnp.exp(s)
        p = p / jnp.sum(p, axis=-1, keepdims=True)
        o_ref[:, sl] = _dot(p.astype(BF16), v_ref[:, sl].astype(BF16)).astype(o_ref.dtype)


def _mem_attend(q3, gain, k3, v3, heads, out_dtype, tq=512):
    b, t, w = q3.shape
    ml = k3.shape[1]
    tq = _tile(t, tq)
    hd = w // heads
    return pl.pallas_call(
        functools.partial(_mem_attend_body, heads=heads),
        grid=(b, t // tq),
        in_specs=[pl.BlockSpec((None, tq, w), lambda i, j: (i, j, 0)),
                  pl.BlockSpec((1, hd), lambda i, j: (0, 0)),
                  pl.BlockSpec((None, ml, w), lambda i, j: (i, 0, 0)),
                  pl.BlockSpec((None, ml, w), lambda i, j: (i, 0, 0))],
        out_specs=pl.BlockSpec((None, tq, w), lambda i, j: (i, j, 0)),
        out_shape=jax.ShapeDtypeStruct((b, t, w), out_dtype),
        compiler_params=_cparams("arbitrary", "arbitrary"),
        name="mem_attend",
    )(q3, gain.reshape(1, hd), k3, v3)


def _log_sigmoid(x):
    return -(jnp.maximum(-x, 0.0) + jnp.log1p(jnp.exp(-jnp.abs(x))))


def _logf_body(x_ref, w_ref, b_ref, o_ref, opad_ref):
    z = _dot(x_ref[...].astype(BF16), w_ref[...].astype(BF16))
    lf = _log_sigmoid(z + b_ref[...])
    opad_ref[...] = lf
    o_ref[...] = lf[:, :o_ref.shape[1]]


def _logf(x, w_pad, b_pad, h, tm=1024):
    m, k = x.shape
    tm = _tile(m, tm)
    return pl.pallas_call(
        _logf_body,
        grid=(m // tm,),
        in_specs=[pl.BlockSpec((tm, k), lambda i: (i, 0)),
                  pl.BlockSpec((k, LANE), lambda i: (0, 0)),
                  pl.BlockSpec((1, LANE), lambda i: (0, 0))],
        out_specs=[pl.BlockSpec((tm, h), lambda i: (i, 0)), pl.BlockSpec((tm, LANE), lambda i: (i, 0))],
        out_shape=[jax.ShapeDtypeStruct((m, h), F32), jax.ShapeDtypeStruct((m, LANE), F32)],
        compiler_params=_cparams("arbitrary"),
        name="logf",
    )(x, w_pad, b_pad)


def _cumsum_body(x_ref, o_ref):
    x = x_ref[...]
    t = x.shape[0]
    row = lax.broadcasted_iota(jnp.int32, x.shape, 0)
    s = 1
    while s < t:
        x = x + jnp.where(row >= s, pltpu.roll(x, s, 0), 0.0)
        s *= 2
    o_ref[...] = x


def _cumsum_time(x3):
    b, t, w = x3.shape
    return pl.pallas_call(
        _cumsum_body,
        grid=(b,),
        in_specs=[pl.BlockSpec((None, t, w), lambda i: (i, 0, 0))],
        out_specs=pl.BlockSpec((None, t, w), lambda i: (i, 0, 0)),
        out_shape=jax.ShapeDtypeStruct((b, t, w), F32),
        compiler_params=_cparams("arbitrary"),
        name="cumsum_time",
    )(x3)


def _fox_prompt_body(q_ref, k_ref, vt_ref, cq_ref, ck_ref, o_ref, ckb_ref, *, group, hd, tk):
    qi = pl.program_id(2)
    tq = q_ref.shape[0]
    cols = group * tq
    scale = hd ** -0.5
    q3 = jnp.concatenate([q_ref[:, g * hd:(g + 1) * hd] for g in range(group)], axis=0)
    cq = jnp.concatenate([cq_ref[g:g + 1, :] for g in range(group)], axis=1)

    @pl.when(qi == 0)
    def _():
        for g in range(group):
            ckb_ref[g] = jnp.broadcast_to(ck_ref[:, g:g + 1], ckb_ref.shape[1:])

    qpos = qi * tq + jnp.concatenate([lax.broadcasted_iota(jnp.int32, (tk, tq), 1)] * group, axis=1)
    krow = lax.broadcasted_iota(jnp.int32, (tk, cols), 0)

    def qk(ki):
        off = pl.multiple_of(ki * tk, tk)
        return _dot_nt(k_ref[pl.ds(off, tk), :].astype(BF16), q3)

    def softmax(ki, qk_raw, m, l, masked):
        off = pl.multiple_of(ki * tk, tk)
        ckb = jnp.concatenate([ckb_ref[g, pl.ds(off, tk), :] for g in range(group)], axis=1)
        s = qk_raw * scale + (cq - ckb)
        if masked:
            s = jnp.where(off + krow <= qpos, s, NEG_INF)
        m_new = jnp.maximum(m, jnp.max(s, axis=0, keepdims=True))
        alpha = jnp.exp(m - m_new)
        p = jnp.exp(s - m_new)
        l = alpha * l + jnp.sum(p, axis=0, keepdims=True)
        return m_new, l, alpha, p.astype(BF16)

    def pv(ki, p, alpha, acc):
        off = pl.multiple_of(ki * tk, tk)
        return alpha * acc + _dot(vt_ref[:, pl.ds(off, tk)].astype(BF16), p)

    def body(ki, carry):
        m, l, acc, qk_raw, p_prev, alpha_prev = carry
        qk_next = qk(ki + 1)
        acc = pv(jnp.maximum(ki - 1, 0), p_prev, alpha_prev, acc)
        m, l, alpha, p = softmax(ki, qk_raw, m, l, False)
        return m, l, acc, qk_next, p, alpha

    init = (jnp.full((1, cols), NEG_INF, F32), jnp.zeros((1, cols), F32), jnp.zeros((hd, cols), F32),
            qk(0), jnp.zeros((tk, cols), BF16), jnp.ones((1, cols), F32))
    n_full = (qi * tq) // tk
    m, l, acc, qk_raw, p_prev, alpha_prev = lax.fori_loop(0, n_full, body, init)
    acc = pv(jnp.maximum(n_full - 1, 0), p_prev, alpha_prev, acc)
    m, l, alpha, p = softmax(n_full, qk_raw, m, l, True)
    acc = pv(n_full, p, alpha, acc)
    out_t = acc / l
    for g in range(group):
        o_ref[:, g * hd:(g + 1) * hd] = out_t[:, g * tq:(g + 1) * tq].T.astype(o_ref.dtype)


def _fox_prompt(qn, k2, vt, c_col, c_row, batch, seq, kvh, group, hd, tk=256):
    m = qn.shape[0]
    tq = _tile(seq, LANE)
    tk = _tile(seq, tk)
    assert tk % tq == 0
    nq = seq // tq
    gw = group * hd
    return pl.pallas_call(
        functools.partial(_fox_prompt_body, group=group, hd=hd, tk=tk),
        grid=(batch, kvh, nq),
        in_specs=[pl.BlockSpec((tq, gw), lambda b, h, i: (b * nq + i, h)),
                  pl.BlockSpec((seq, hd), lambda b, h, i: (b, h)),
                  pl.BlockSpec((None, None, hd, seq), lambda b, h, i: (b, h, 0, 0)),
                  pl.BlockSpec((None, None, group, tq), lambda b, h, i: (b, h, 0, i)),
                  pl.BlockSpec((None, None, seq, group), lambda b, h, i: (b, h, 0, 0))],
        out_specs=pl.BlockSpec((tq, gw), lambda b, h, i: (b * nq + i, h)),
        out_shape=jax.ShapeDtypeStruct((m, kvh * gw), BF16),
        scratch_shapes=[pltpu.VMEM((group, seq, tq), F32)],
        compiler_params=_cparams("arbitrary", "arbitrary", "arbitrary"),
        name="fox_prompt",
    )(qn, k2, vt, c_row, c_col)


def _fox_decode_body(pt_ref, q_ref, kn_ref, vn_ref, lfn_ref, *refs, kvh, group, hd, pages):
    k_refs, v_refs, lf_refs = refs[:pages], refs[pages:2 * pages], refs[2 * pages:3 * pages]
    o_ref, m_ref, l_ref, acc_ref, d_ref = refs[3 * pages:]
    p = pl.program_id(1)
    heads = kvh * group
    ps = lf_refs[0].shape[1]
    scale = hd ** -0.5
    head_kv = lax.broadcasted_iota(jnp.int32, (heads, hd), 0) // group
    q = q_ref[...]

    def per_head(row_ref):
        out = jnp.zeros((heads, hd), F32)
        for j in range(kvh):
            out = out + jnp.where(head_kv == j, row_ref[:, j * hd:(j + 1) * hd], 0.0)
        return out

    @pl.when(p == 0)
    def _():
        m_ref[...] = jnp.sum(q.astype(F32) * per_head(kn_ref), axis=1, keepdims=True) * scale
        l_ref[...] = jnp.ones_like(l_ref)
        acc_ref[...] = per_head(vn_ref)
        d_ref[...] = lfn_ref[...]

    col_kv = lax.broadcasted_iota(jnp.int32, (heads, kvh * ps), 1) // ps
    own = col_kv == lax.broadcasted_iota(jnp.int32, (heads, kvh * ps), 0) // group
    lane = lax.broadcasted_iota(jnp.int32, (heads, ps), 1)
    qb = q.astype(BF16)
    d = d_ref[...]
    scores, values = [], []
    for k_ref, v_ref, lf_ref in zip(k_refs, v_refs, lf_refs):
        lf = lf_ref[...]
        suf = lf
        s = 1
        while s < ps:
            suf = suf + jnp.where(lane < ps - s, pltpu.roll(suf, ps - s, 1), 0.0)
            s *= 2
        bias = d + (suf - lf)
        d = d + suf[:, 0:1]
        kcat = jnp.concatenate([k_ref[pl.ds(j, ps, stride=kvh), :] for j in range(kvh)], axis=0).astype(BF16)
        values.append(jnp.concatenate([v_ref[pl.ds(j, ps, stride=kvh), :] for j in range(kvh)],
                                      axis=0).astype(BF16))
        sc = _dot_nt(qb, kcat) * scale + jnp.concatenate([bias] * kvh, axis=1)
        scores.append(jnp.where(own, sc, NEG_INF))
    d_ref[...] = d
    m_old = m_ref[...]
    m_new = m_old
    for sc in scores:
        m_new = jnp.maximum(m_new, jnp.max(sc, axis=-1, keepdims=True))
    alpha = jnp.exp(m_old - m_new)
    l_new = alpha * l_ref[...]
    acc = alpha * acc_ref[...]
    for sc, vcat in zip(scores, values):
        pr = jnp.exp(sc - m_new)
        l_new = l_new + jnp.sum(pr, axis=-1, keepdims=True)
        acc = acc + _dot(pr.astype(BF16), vcat)
    l_ref[...] = l_new
    acc_ref[...] = acc
    m_ref[...] = m_new

    @pl.when(p == pl.num_programs(1) - 1)
    def _():
        o_ref[...] = acc_ref[...] / l_ref[...]


def _fox_decode(page_table, q3, k_new, v_new, lf_new, cache_k, cache_v, cache_lf_t, kvh, group, hd, pages=8):
    db, heads, _ = q3.shape
    npg = page_table.shape[1]
    ps = cache_lf_t.shape[2]
    w = kvh * hd
    while npg % pages:
        pages -= 1

    def page3(i):
        return lambda b, p, pt: (pt[b, npg - 1 - (p * pages + i)], 0, 0)

    grid_spec = pltpu.PrefetchScalarGridSpec(
        num_scalar_prefetch=1,
        grid=(db, npg // pages),
        in_specs=[pl.BlockSpec((None, heads, hd), lambda b, p, pt: (b, 0, 0)),
                  pl.BlockSpec((None, 1, w), lambda b, p, pt: (b, 0, 0)),
                  pl.BlockSpec((None, 1, w), lambda b, p, pt: (b, 0, 0)),
                  pl.BlockSpec((None, heads, 1), lambda b, p, pt: (b, 0, 0))]
                 + [pl.BlockSpec((None, ps * kvh, hd), page3(i)) for i in range(pages)]
                 + [pl.BlockSpec((None, ps * kvh, hd), page3(i)) for i in range(pages)]
                 + [pl.BlockSpec((None, heads, ps), page3(i)) for i in range(pages)],
        out_specs=pl.BlockSpec((None, heads, hd), lambda b, p, pt: (b, 0, 0)),
        scratch_shapes=[pltpu.VMEM((heads, 1), F32), pltpu.VMEM((heads, 1), F32),
                        pltpu.VMEM((heads, hd), F32), pltpu.VMEM((heads, 1), F32)],
    )
    return pl.pallas_call(
        functools.partial(_fox_decode_body, kvh=kvh, group=group, hd=hd, pages=pages),
        grid_spec=grid_spec,
        out_shape=jax.ShapeDtypeStruct((db, heads, hd), F32),
        compiler_params=_cparams("arbitrary", "arbitrary"),
        name="fox_decode",
    )(page_table, q3, k_new, v_new, lf_new, *([cache_k] * pages), *([cache_v] * pages), *([cache_lf_t] * pages))


def _router_body(x_ref, g_ref, w_ref, b_ref, hn_ref, idx_ref, gate_ref, *, n_exp):
    y = _rms(x_ref[...], g_ref[...])
    tm, groups = y.shape[0], y.shape[1] // LANE
    for c in range(groups):
        hn_ref[pl.ds(c, tm, stride=groups), :] = y[:, c * LANE:(c + 1) * LANE]
    logits = jnp.dot(y, w_ref[...], preferred_element_type=F32,
                     precision=lax.Precision.HIGHEST) + b_ref[...]
    lane = lax.broadcasted_iota(jnp.int32, logits.shape, 1)
    logits = jnp.where(lane < n_exp, logits, NEG_INF)
    m1 = jnp.max(logits, axis=-1, keepdims=True)
    i1 = jnp.min(jnp.where(logits == m1, lane, LANE), axis=-1, keepdims=True)
    rest = jnp.where(lane == i1, NEG_INF, logits)
    m2 = jnp.max(rest, axis=-1, keepdims=True)
    i2 = jnp.min(jnp.where(rest == m2, lane, LANE), axis=-1, keepdims=True)
    e = jnp.exp(m2 - m1)
    g1 = 1.0 / (1.0 + e)
    g2 = e / (1.0 + e)
    idx_ref[...] = jnp.where(lane == 0, i1, jnp.where(lane == 1, i2, 0))
    gate_ref[...] = jnp.where(lane == 0, g1, jnp.where(lane == 1, g2, 0.0))


def _router(x, gain, w_pad, b_pad, n_exp):
    m, k = x.shape
    tm = _tile(m, 256)
    groups = k // LANE
    return pl.pallas_call(
        functools.partial(_router_body, n_exp=n_exp),
        grid=(m // tm,),
        in_specs=[pl.BlockSpec((tm, k), lambda i: (i, 0)),
                  pl.BlockSpec((1, k), lambda i: (0, 0)),
                  pl.BlockSpec((k, LANE), lambda i: (0, 0)),
                  pl.BlockSpec((1, LANE), lambda i: (0, 0))],
        out_specs=[pl.BlockSpec((tm * groups, LANE), lambda i: (i, 0)),
                   pl.BlockSpec((tm, LANE), lambda i: (i, 0)), pl.BlockSpec((tm, LANE), lambda i: (i, 0))],
        out_shape=[jax.ShapeDtypeStruct((m * groups, LANE), F32),
                   jax.ShapeDtypeStruct((m, LANE), jnp.int32), jax.ShapeDtypeStruct((m, LANE), F32)],
        compiler_params=_cparams("arbitrary"),
        name="router",
    )(x, gain.reshape(1, k), w_pad, b_pad)


def _gather_body(tok_ref, used_ref, a_ref, b_ref, o_ref, buf_ref, sem, *, tg, groups):
    i = pl.program_id(0)
    ma = a_ref.shape[0] // groups
    slot = i % 2

    def row_copy(src_ref, t, r, s):
        return pltpu.make_async_copy(src_ref.at[pl.ds(pl.multiple_of(t * groups, groups), groups)],
                                     buf_ref.at[s, pl.ds(pl.multiple_of(r * groups, groups), groups)], sem.at[s])

    def issue_tile(tile, s):
        base = tile * tg

        @pl.when(base < used_ref[0])
        def _():
            def issue(r8, c):
                for u in range(8):
                    r = r8 * 8 + u
                    t = tok_ref[base + r]

                    @pl.when(t < ma)
                    def _():
                        row_copy(a_ref, t, r, s).start(priority=u % 2)

                    @pl.when(t >= ma)
                    def _():
                        row_copy(b_ref, t - ma, r, s).start(priority=u % 2)
                return c

            lax.fori_loop(0, tg // 8, issue, 0)

    @pl.when(i == 0)
    def _():
        issue_tile(0, 0)

    @pl.when(i + 1 < pl.num_programs(0))
    def _():
        issue_tile(i + 1, 1 - slot)

    @pl.when(i * tg < used_ref[0])
    def _():
        def drain(r, c):
            row_copy(a_ref, 0, r, slot).wait()
            return c

        lax.fori_loop(0, tg, drain, 0, unroll=8)
        for c in range(groups):
            o_ref[:, c * LANE:(c + 1) * LANE] = buf_ref[slot, pl.ds(c, tg, stride=groups), :].astype(o_ref.dtype)

    @pl.when(i * tg >= used_ref[0])
    def _():
        o_ref[...] = jnp.zeros_like(o_ref)


def _gather_rows(row_token, used_rows, src_a, src_b, d, tg=256):
    p = row_token.shape[0]
    groups = d // LANE
    tg = _tile(p, tg)
    grid_spec = pltpu.PrefetchScalarGridSpec(
        num_scalar_prefetch=2,
        grid=(p // tg,),
        in_specs=[pl.BlockSpec(memory_space=pl.ANY), pl.BlockSpec(memory_space=pl.ANY)],
        out_specs=pl.BlockSpec((tg, d), lambda i, tok, used: (i, 0)),
        scratch_shapes=[pltpu.VMEM((2, tg * groups, LANE), F32), pltpu.SemaphoreType.DMA((2,))],
    )
    return pl.pallas_call(
        functools.partial(_gather_body, tg=tg, groups=groups),
        grid_spec=grid_spec,
        out_shape=jax.ShapeDtypeStruct((p, d), BF16),
        compiler_params=_cparams("arbitrary"),
        name="moe_gather",
    )(row_token, used_rows, src_a, src_b)


def _combine_body(pos_ref, y_ref, h_ref, g_ref, o_ref, buf_ref, sem, *, tt):
    i = pl.program_id(0)
    base = i * tt

    def row_copy(k, src_row, r):
        return pltpu.make_async_copy(y_ref.at[pl.ds(src_row, 1)], buf_ref.at[k, pl.ds(r, 1)], sem)

    def issue(r, c):
        row_copy(0, pos_ref[2 * (base + r)], r).start(priority=0)
        row_copy(1, pos_ref[2 * (base + r) + 1], r).start(priority=1)
        return c

    lax.fori_loop(0, tt, issue, 0)

    def drain(r, c):
        row_copy(0, 0, r).wait()
        row_copy(1, 0, r).wait()
        return c

    lax.fori_loop(0, tt, drain, 0)
    g = g_ref[...]
    o_ref[...] = h_ref[...] + (g[:, 0:1] * buf_ref[0] + g[:, 1:2] * buf_ref[1])


def _combine(pos_flat, y_sorted, h, gates, tt=256):
    m, d = h.shape
    tt = _tile(m, tt)
    grid_spec = pltpu.PrefetchScalarGridSpec(
        num_scalar_prefetch=1,
        grid=(m // tt,),
        in_specs=[pl.BlockSpec(memory_space=pl.ANY),
                  pl.BlockSpec((tt, d), lambda i, pos: (i, 0)),
                  pl.BlockSpec((tt, LANE), lambda i, pos: (i, 0))],
        out_specs=pl.BlockSpec((tt, d), lambda i, pos: (i, 0)),
        scratch_shapes=[pltpu.VMEM((2, tt, d), F32), pltpu.SemaphoreType.DMA(())],
    )
    return pl.pallas_call(
        functools.partial(_combine_body, tt=tt),
        grid_spec=grid_spec,
        out_shape=jax.ShapeDtypeStruct((m, d), F32),
        compiler_params=_cparams("arbitrary"),
        name="moe_combine",
    )(pos_flat, y_sorted, h, gates)


def _for_row_blocks(nrows, total, big, small, live, dead=None):
    def run(fn, size, lo, hi):
        def body(t, carry):
            fn(pl.ds(pl.multiple_of(t * size, size), size), size)
            return carry
        lax.fori_loop(lo, hi, body, 0)

    end = (nrows + small - 1) // small
    nbig = end // (big // small)
    run(live, big, 0, nbig)
    done = nbig * (big // small)
    run(live, small, done, end)
    if dead is not None:
        run(dead, small, end, total // small)


def _moe_glu_body(ce_ref, cx_ref, cv_ref, used_ref, x_ref, wg_ref, wu_ref, o_ref, *, big, small):
    c = pl.program_id(0)

    @pl.when(c < used_ref[0])
    def _():
        def live(rows, size):
            x = x_ref[rows, :]
            g = _dot(x, wg_ref[...].astype(BF16))
            u = _dot(x, wu_ref[...].astype(BF16))
            o_ref[rows, :] = _silu_mul(g, u).astype(o_ref.dtype)

        def dead(rows, size):
            o_ref[rows, :] = jnp.zeros((size, o_ref.shape[1]), o_ref.dtype)

        _for_row_blocks(cv_ref[c], x_ref.shape[0], big, small, live, dead)

    @pl.when(c >= used_ref[0])
    def _():
        o_ref[...] = jnp.zeros_like(o_ref)


def _moe_glu(sched, x_sorted, we_gu, chunk, big, small, tn=256):
    ce, cx, cv, used = sched
    p, d = x_sorted.shape
    f = we_gu.shape[2] // 2
    tn = _tile(f, tn)
    nj = f // tn
    nc = p // chunk

    def jpin(c, j, used):
        return jnp.where(c < used[0], j, nj - 1)

    grid_spec = pltpu.PrefetchScalarGridSpec(
        num_scalar_prefetch=4,
        grid=(nc, nj),
        in_specs=[pl.BlockSpec((chunk, d), lambda c, j, ce, cx, cv, used: (cx[c], 0),
                               pipeline_mode=pl.Buffered(1)),
                  pl.BlockSpec((None, d, tn), lambda c, j, ce, cx, cv, used: (ce[c], 0, jpin(c, j, used))),
                  pl.BlockSpec((None, d, tn), lambda c, j, ce, cx, cv, used: (ce[c], 0, nj + jpin(c, j, used)))],
        out_specs=pl.BlockSpec((chunk, tn), lambda c, j, ce, cx, cv, used: (c, j)),
    )
    return pl.pallas_call(
        functools.partial(_moe_glu_body, big=big, small=small),
        grid_spec=grid_spec,
        out_shape=jax.ShapeDtypeStruct((p, f), BF16),
        compiler_params=_cparams("arbitrary", "arbitrary"),
        name="moe_glu",
    )(ce, cx, cv, used, x_sorted, we_gu, we_gu)


def _moe_down_body(ce_ref, cx_ref, cv_ref, used_ref, x_ref, w_ref, o_ref, *, big, small):
    c, kk = pl.program_id(0), pl.program_id(2)

    @pl.when(c < used_ref[0])
    def _():
        def first(rows, size):
            o_ref[rows, :] = _dot(x_ref[rows, :], w_ref[...].astype(BF16))

        def accumulate(rows, size):
            o_ref[rows, :] += _dot(x_ref[rows, :], w_ref[...].astype(BF16))

        def dead(rows, size):
            o_ref[rows, :] = jnp.zeros((size, o_ref.shape[1]), o_ref.dtype)

        @pl.when(kk == 0)
        def _():
            _for_row_blocks(cv_ref[c], x_ref.shape[0], big, small, first, dead)

        @pl.when(kk > 0)
        def _():
            _for_row_blocks(cv_ref[c], x_ref.shape[0], big, small, accumulate)

    @pl.when((c >= used_ref[0]) & (kk == 0))
    def _():
        o_ref[...] = jnp.zeros_like(o_ref)


def _moe_down(sched, act_sorted, we_down, chunk, big, small, tn=512, tk=2048):
    ce, cx, cv, used = sched
    p, f = act_sorted.shape
    d = we_down.shape[2]
    tn, tk = _tile(d, tn), _tile(f, tk)
    nj, nk = d // tn, f // tk
    nc = p // chunk

    def pin(c, v, last, used):
        return jnp.where(c < used[0], v, last)

    grid_spec = pltpu.PrefetchScalarGridSpec(
        num_scalar_prefetch=4,
        grid=(nc, nj, nk),
        in_specs=[pl.BlockSpec((chunk, tk), lambda c, j, k, ce, cx, cv, used: (cx[c], pin(c, k, nk - 1, used))),
                  pl.BlockSpec((None, tk, tn), lambda c, j, k, ce, cx, cv, used:
                               (ce[c], pin(c, k, nk - 1, used), pin(c, j, nj - 1, used)))],
        out_specs=pl.BlockSpec((chunk, tn), lambda c, j, k, ce, cx, cv, used: (c, j)),
    )
    return pl.pallas_call(
        functools.partial(_moe_down_body, big=big, small=small),
        grid_spec=grid_spec,
        out_shape=jax.ShapeDtypeStruct((p, d), F32),
        compiler_params=_cparams("arbitrary", "arbitrary", "arbitrary"),
        name="moe_down",
    )(ce, cx, cv, used, act_sorted, we_down)


def _moe(h_p, h_s, gain, w_router, b_router, we_gu, we_down):
    mp, d = h_p.shape
    ms = h_s.shape[0]
    n_exp = we_gu.shape[0]
    mt = mp + ms
    w_pad = jnp.pad(w_router, ((0, 0), (0, LANE - n_exp)))
    b_pad = jnp.pad(b_router.reshape(1, n_exp).astype(F32), ((0, 0), (0, LANE - n_exp)))
    hn_p, idx_p, gate_p = _router(h_p, gain, w_pad, b_pad, n_exp)
    hn_s, idx_s, gate_s = _router(h_s, gain, w_pad, b_pad, n_exp)

    chunk = min(MOE_CHUNK, -(-2 * mt // MOE_SMALL) * MOE_SMALL)
    big = min(MOE_BIG, chunk)
    idx = jnp.concatenate([idx_p[:, :2], idx_s[:, :2]], axis=0)
    sel = (idx[:, :, None] == jnp.arange(n_exp, dtype=jnp.int32)[None, None, :]).any(axis=1).astype(jnp.int32)
    rank = jnp.cumsum(sel, axis=0) - sel
    cnt = jnp.sum(sel, axis=0)
    nch = (cnt + chunk - 1) // chunk
    ch_end = jnp.cumsum(nch)
    ch_off = ch_end - nch
    n_chunks = (2 * mt + n_exp * (chunk - 1)) // chunk
    p_rows = n_chunks * chunk
    pos = jnp.take_along_axis(ch_off[None, :] * chunk + rank, idx, axis=1)
    tok = jnp.repeat(jnp.arange(mt, dtype=jnp.int32), 2)
    row_token = jnp.zeros((p_rows,), jnp.int32).at[pos.reshape(-1)].set(tok)
    used = ch_end[-1].astype(jnp.int32)
    cid = jnp.minimum(jnp.arange(n_chunks, dtype=jnp.int32), used - 1)
    ce = jnp.minimum(jnp.sum((ch_end[None, :] <= cid[:, None]).astype(jnp.int32), axis=1), n_exp - 1)
    cv = jnp.clip(cnt[ce] - (cid - ch_off[ce]) * chunk, 0, chunk).astype(jnp.int32)
    sched = (ce, cid, cv, used.reshape(1))

    x_sorted = _gather_rows(row_token, (used * chunk).reshape(1), hn_p, hn_s, d)
    act = _moe_glu(sched, x_sorted, we_gu, chunk, big, MOE_SMALL)
    y_sorted = _moe_down(sched, act, we_down, chunk, big, MOE_SMALL)
    pos = pos.astype(jnp.int32)
    out_p = _combine(pos[:mp].reshape(-1), y_sorted, h_p, gate_p)
    out_s = _combine(pos[mp:].reshape(-1), y_sorted, h_s, gate_s)
    return out_p, out_s


def kernel(x_prompt, x_sample, mem_prompt, cache_k, cache_v, cache_logf, state_conv, cache_mem_k, cache_mem_v, page_table, norm_mix, norm_ffn, w_in_a, conv_w, w_in_b, q_norm_b, w_out, w_mem_kv, mem_q_norm, mem_k_norm, kv_norm, w_kv, b_f, k_norm, w_gu, w_down, w_router, b_router, we_gu, we_down):
    bsz, seq, d = x_prompt.shape
    db, dec_seq, _ = x_sample.shape
    assert dec_seq == 1, "the decode path handles one new token per sequence"
    depth, _, ml, mem_heads, mem_hd = cache_mem_k.shape
    assert depth == 2 and w_in_a.shape[0] == 1 and w_in_b.shape[0] == 1
    mem_w = mem_heads * mem_hd
    n_pool, ps, kvh, hd = cache_k.shape
    heads = cache_logf.shape[2]
    group = heads // kvh
    kv_w = kvh * hd
    cd = conv_w.shape[2]
    assert conv_w.shape[1] == 3 and state_conv.shape[2] == 2
    mp = bsz * seq

    mem_bf = mem_prompt.reshape(bsz * ml, d).astype(BF16)
    mk, mv = [], []
    for i in range(depth):
        mk.append(_mm_headnorm(mem_bf, w_mem_kv, i, 0, mem_w, mem_k_norm[i], mem_hd, F32))
        mv.append(_mm(mem_bf, w_mem_kv, i, mem_w, mem_w, F32))
    mem_k_prompt = jnp.stack(mk).reshape(depth, bsz, ml, mem_heads, mem_hd)
    mem_v_prompt = jnp.stack(mv).reshape(depth, bsz, ml, mem_heads, mem_hd)

    w_kv3 = w_kv.reshape(1, d, -1)
    w_lf = jnp.pad(w_kv[:, 2 * kv_w:], ((0, 0), (0, LANE - heads)))
    b_lf = jnp.pad(b_f.reshape(1, heads).astype(F32), ((0, 0), (0, LANE - heads)))
    we_gu3 = we_gu.reshape(we_gu.shape[1:])
    we_down3 = we_down.reshape(we_down.shape[1:])

    def layer0_tail(h0, main, q_mem, mem_k0, mem_v0, nb, act_dtype):
        t = h0.shape[0] // nb
        mo = _mem_attend(q_mem.reshape(nb, t, mem_w), mem_q_norm[0], mem_k0.reshape(nb, ml, mem_w),
                         mem_v0.reshape(nb, ml, mem_w), mem_heads, act_dtype).reshape(nb * t, mem_w)
        h1 = _mm_out(main, mo, w_out, 0, h0)
        (hn,) = _rmsnorm(h1, norm_ffn[0], (act_dtype,))
        act = _mm_glu(hn, w_gu, 0, act_dtype)
        h2 = _mm_down(act, w_down, 0, h1)
        (kvn,) = _rmsnorm(h2, kv_norm, (act_dtype,))
        k = _mm_headnorm(kvn, w_kv3, 0, 0, kv_w, k_norm, hd, F32)
        v = _mm(kvn, w_kv3, 0, kv_w, kv_w, F32)
        logf, logf_pad = _logf(kvn, w_lf, b_lf, heads)
        (xn1,) = _rmsnorm(h2, norm_mix[1], (act_dtype,))
        qn = _mm_headnorm(xn1, w_in_b, 0, 0, heads * hd, q_norm_b[0], hd, act_dtype)
        q_mem1 = _mm(xn1, w_in_b, 0, heads * hd, mem_w, F32)
        return h2, k, v, logf, logf_pad, qn, q_mem1

    def layer1_attn_out(h2, attn, q_mem1, mem_k1, mem_v1, nb, act_dtype):
        t = h2.shape[0] // nb
        mo = _mem_attend(q_mem1.reshape(nb, t, mem_w), mem_q_norm[1], mem_k1.reshape(nb, ml, mem_w),
                         mem_v1.reshape(nb, ml, mem_w), mem_heads, act_dtype).reshape(nb * t, mem_w)
        return _mm_out(attn, mo, w_out, 1, h2)

    xp = x_prompt.reshape(mp, d)
    (xn,) = _rmsnorm(xp, norm_mix[0], (BF16,))
    main_p, conv_tail = _conv_prompt(xn, w_in_a, conv_w, 0, bsz, seq)
    q_mem_p = _mm(xn, w_in_a, 0, 3 * cd, mem_w, F32)
    h2_p, k_p, v_p, logf_p, logf_pad_p, qn_p, q_mem1_p = layer0_tail(
        xp, main_p, q_mem_p, mk[0], mv[0], bsz, BF16)
    c = _cumsum_time(logf_pad_p.reshape(bsz, seq, LANE))[:, :, :heads].reshape(bsz, seq, kvh, group)
    vt_p = v_p.reshape(bsz, seq, kvh, hd).transpose(0, 2, 3, 1)
    attn_p = _fox_prompt(qn_p, k_p, vt_p, c.transpose(0, 2, 1, 3), c.transpose(0, 2, 3, 1),
                         bsz, seq, kvh, group, hd)
    h3_p = layer1_attn_out(h2_p, attn_p, q_mem1_p, mk[1], mv[1], bsz, BF16)

    xs = x_sample.reshape(db, d)
    (xn_s,) = _rmsnorm(xs, norm_mix[0], (F32,))
    z_s = _mm(xn_s, w_in_a, 0, 0, 3 * cd + mem_w, F32)
    main_s, conv_new_s = _conv_step(z_s, state_conv[0], conv_w[0])
    h2_s, k_s, v_s, logf_s, _, qn_s, q_mem1_s = layer0_tail(
        xs, main_s, z_s[:, 3 * cd:], cache_mem_k[0], cache_mem_v[0], db, F32)
    attn_s = _fox_decode(page_table, qn_s.reshape(db, heads, hd), k_s.reshape(db, 1, kv_w),
                         v_s.reshape(db, 1, kv_w), logf_s.reshape(db, heads, 1),
                         cache_k.reshape(n_pool, ps * kvh, hd), cache_v.reshape(n_pool, ps * kvh, hd),
                         cache_logf.transpose(0, 2, 1), kvh, group, hd).reshape(db, heads * hd)
    h3_s = layer1_attn_out(h2_s, attn_s, q_mem1_s, cache_mem_k[1], cache_mem_v[1], db, F32)

    y_p, y_s = _moe(h3_p, h3_s, norm_ffn[1], w_router[0], b_router[0], we_gu3, we_down3)

    return (y_p.reshape(bsz, seq, d), y_s.reshape(db, 1, d),
            k_p.reshape(bsz, seq, kvh, hd), v_p.reshape(bsz, seq, kvh, hd), logf_p.reshape(bsz, seq, heads),
            conv_tail.reshape(1, bsz, 2, cd), mem_k_prompt, mem_v_prompt,
            k_s.reshape(db, 1, kvh, hd), v_s.reshape(db, 1, kvh, hd), logf_s.reshape(db, 1, heads),
            conv_new_s.reshape(1, db, 2, cd))
```

```python
import functools

import jax
import jax.numpy as jnp
from jax import lax
from jax.experimental import pallas as pl
from jax.experimental.pallas import tpu as pltpu

F32 = jnp.float32
BF16 = jnp.bfloat16
EPS = 1e-6
NEG_INF = float("-inf")

V7X_VMEM_BYTES = 64 * 1024 * 1024
VMEM_LIMIT = V7X_VMEM_BYTES - 8 * 1024 * 1024
LANE = 128
MOE_CHUNK = 3328
MOE_BLOCKS = (1024, 512, 128)


def _cparams(*sem):
    return pltpu.CompilerParams(dimension_semantics=sem, vmem_limit_bytes=VMEM_LIMIT)


def _tile(n, pref):
    if n <= pref:
        return n
    t = pref
    while n % t:
        t -= LANE
    assert t > 0
    return t


def _dot(a, b):
    return jnp.dot(a, b, preferred_element_type=F32)


def _dot_nt(a, b):
    return lax.dot_general(a, b, (((1,), (1,)), ((), ())), preferred_element_type=F32)


def _rms(x, g):
    return x * lax.rsqrt(jnp.mean(x * x, axis=-1, keepdims=True) + EPS) * g


def _rmsnorm_body(x_ref, g_ref, *o_refs):
    y = _rms(x_ref[...].astype(F32), g_ref[...])
    for o in o_refs:
        o[...] = y.astype(o.dtype)


def _rmsnorm(x, g, dtypes):
    m, d = x.shape
    tr = _tile(m, 256)
    outs = pl.pallas_call(
        _rmsnorm_body,
        grid=(m // tr,),
        in_specs=[pl.BlockSpec((tr, d), lambda i: (i, 0)), pl.BlockSpec((1, d), lambda i: (0, 0))],
        out_specs=[pl.BlockSpec((tr, d), lambda i: (i, 0)) for _ in dtypes],
        out_shape=[jax.ShapeDtypeStruct((m, d), dt) for dt in dtypes],
        compiler_params=_cparams("arbitrary"),
        name="rmsnorm",
    )(x, g.reshape(1, d))
    return outs


def _mm_body(x_ref, w_ref, o_ref):
    o_ref[...] = _dot(x_ref[...].astype(BF16), w_ref[...].astype(BF16)).astype(o_ref.dtype)


def _mm(x, w3, layer, c0, n, out_dtype, tm=1024, tn=512):
    m, k = x.shape
    tm, tn = _tile(m, tm), _tile(n, tn)
    assert c0 % tn == 0
    return pl.pallas_call(
        _mm_body,
        grid=(m // tm, n // tn),
        in_specs=[pl.BlockSpec((tm, k), lambda i, j: (i, 0)),
                  pl.BlockSpec((None, k, tn), lambda i, j: (layer, 0, c0 // tn + j))],
        out_specs=pl.BlockSpec((tm, tn), lambda i, j: (i, j)),
        out_shape=jax.ShapeDtypeStruct((m, n), out_dtype),
        compiler_params=_cparams("arbitrary", "arbitrary"),
        name="mm",
    )(x, w3)


def _mm_headnorm_body(x_ref, w_ref, g_ref, o_ref, *, hd):
    z = _dot(x_ref[...].astype(BF16), w_ref[...].astype(BF16))
    for h in range(z.shape[1] // hd):
        o_ref[:, h * hd:(h + 1) * hd] = _rms(z[:, h * hd:(h + 1) * hd], g_ref[...]).astype(o_ref.dtype)


def _mm_headnorm(x, w3, layer, c0, n, gain, hd, out_dtype, tm=1024, tn=512):
    m, k = x.shape
    tm, tn = _tile(m, tm), _tile(n, tn)
    assert c0 % tn == 0 and tn % hd == 0
    return pl.pallas_call(
        functools.partial(_mm_headnorm_body, hd=hd),
        grid=(m // tm, n // tn),
        in_specs=[pl.BlockSpec((tm, k), lambda i, j: (i, 0)),
                  pl.BlockSpec((None, k, tn), lambda i, j: (layer, 0, c0 // tn + j)),
                  pl.BlockSpec((1, hd), lambda i, j: (0, 0))],
        out_specs=pl.BlockSpec((tm, tn), lambda i, j: (i, j)),
        out_shape=jax.ShapeDtypeStruct((m, n), out_dtype),
        compiler_params=_cparams("arbitrary", "arbitrary"),
        name="mm_headnorm",
    )(x, w3, gain.reshape(1, hd))


def _mm_out_body(x1_ref, x2_ref, w_ref, r_ref, o_ref):
    k1 = x1_ref.shape[1]
    w = w_ref[...].astype(BF16)
    acc = _dot(x1_ref[...].astype(BF16), w[:k1]) + _dot(x2_ref[...].astype(BF16), w[k1:])
    o_ref[...] = r_ref[...] + acc


def _mm_out(x1, x2, w3, layer, resid, tm=2048, tn=512):
    m, k1 = x1.shape
    k2 = x2.shape[1]
    n = w3.shape[2]
    tm, tn = _tile(m, tm), _tile(n, tn)
    once = dict(pipeline_mode=pl.Buffered(1))
    return pl.pallas_call(
        _mm_out_body,
        grid=(m // tm, n // tn),
        in_specs=[pl.BlockSpec((tm, k1), lambda i, j: (i, 0), **once),
                  pl.BlockSpec((tm, k2), lambda i, j: (i, 0), **once),
                  pl.BlockSpec((None, k1 + k2, tn), lambda i, j: (layer, 0, j)),
                  pl.BlockSpec((tm, tn), lambda i, j: (i, j))],
        out_specs=pl.BlockSpec((tm, tn), lambda i, j: (i, j)),
        out_shape=jax.ShapeDtypeStruct((m, n), F32),
        compiler_params=_cparams("arbitrary", "arbitrary"),
        name="mm_out",
    )(x1, x2, w3, resid)


def _silu_mul(g, u):
    return g * jax.nn.sigmoid(g) * u


def _mm_glu_body(x_ref, wg_ref, wu_ref, o_ref):
    x = x_ref[...].astype(BF16)
    g = _dot(x, wg_ref[...].astype(BF16))
    u = _dot(x, wu_ref[...].astype(BF16))
    o_ref[...] = _silu_mul(g, u).astype(o_ref.dtype)


def _mm_glu(x, w3, layer, out_dtype, tm=1024, tn=256):
    m, k = x.shape
    f = w3.shape[2] // 2
    tm, tn = _tile(m, tm), _tile(f, tn)
    return pl.pallas_call(
        _mm_glu_body,
        grid=(m // tm, f // tn),
        in_specs=[pl.BlockSpec((tm, k), lambda i, j: (i, 0)),
                  pl.BlockSpec((None, k, tn), lambda i, j: (layer, 0, j)),
                  pl.BlockSpec((None, k, tn), lambda i, j: (layer, 0, f // tn + j))],
        out_specs=pl.BlockSpec((tm, tn), lambda i, j: (i, j)),
        out_shape=jax.ShapeDtypeStruct((m, f), out_dtype),
        compiler_params=_cparams("arbitrary", "arbitrary"),
        name="mm_glu",
    )(x, w3, w3)


def _mm_down_body(x_ref, w_ref, r_ref, o_ref):
    kk = pl.program_id(2)
    part = _dot(x_ref[...].astype(BF16), w_ref[...].astype(BF16))

    @pl.when(kk == 0)
    def _():
        o_ref[...] = part

    @pl.when(kk > 0)
    def _():
        o_ref[...] += part

    @pl.when(kk == pl.num_programs(2) - 1)
    def _():
        o_ref[...] = r_ref[...] + o_ref[...]


def _mm_down(x, w3, layer, resid, tm=2048, tn=512, tk=2048):
    m, k = x.shape
    n = w3.shape[2]
    tm, tn, tk = _tile(m, tm), _tile(n, tn), _tile(k, tk)
    return pl.pallas_call(
        _mm_down_body,
        grid=(m // tm, n // tn, k // tk),
        in_specs=[pl.BlockSpec((tm, tk), lambda i, j, kk: (i, kk)),
                  pl.BlockSpec((None, tk, tn), lambda i, j, kk: (layer, kk, j)),
                  pl.BlockSpec((tm, tn), lambda i, j, kk: (i, j))],
        out_specs=pl.BlockSpec((tm, tn), lambda i, j, kk: (i, j)),
        out_shape=jax.ShapeDtypeStruct((m, n), F32),
        compiler_params=_cparams("arbitrary", "arbitrary", "arbitrary"),
        name="mm_down",
    )(x, w3, resid)


def _conv_prompt_body(x_ref, wb_ref, wc_ref, wh_ref, cw_ref, y_ref, tail_ref, carry_ref, *, tiles_per_seq):
    i, j = pl.program_id(0), pl.program_id(1)
    x = x_ref[...]
    b = _dot(x, wb_ref[...].astype(BF16))
    c = _dot(x, wc_ref[...].astype(BF16))
    hv = _dot(x, wh_ref[...].astype(BF16))
    u = c * hv
    tm = u.shape[0]
    @pl.when(i % tiles_per_seq == 0)
    def _():
        carry_ref[j] = jnp.zeros(carry_ref.shape[1:], F32)

    prev = carry_ref[j]
    row = lax.broadcasted_iota(jnp.int32, u.shape, 0)
    u1 = jnp.where(row == 0, prev[1:2], pltpu.roll(u, 1, 0))
    u2 = jnp.where(row == 0, prev[0:1], jnp.where(row == 1, prev[1:2], pltpu.roll(u, 2, 0)))
    cw = cw_ref[...]
    y = cw[0:1] * u2 + cw[1:2] * u1 + cw[2:3] * u
    y_ref[...] = (b * y).astype(y_ref.dtype)
    carry_ref[j] = u[tm - 2:tm]
    tail_ref[...] = u[tm - 2:tm]


def _conv_prompt(xn, w_in_a, conv_w, layer, batch, seq, tm=1024, tn=256):
    m, d = xn.shape
    cd = conv_w.shape[2]
    tm, tn = _tile(seq, tm), _tile(cd, tn)
    nj = cd // tn
    tps = seq // tm
    y, tail = pl.pallas_call(
        functools.partial(_conv_prompt_body, tiles_per_seq=tps),
        grid=(m // tm, nj),
        in_specs=[pl.BlockSpec((tm, d), lambda i, j: (i, 0)),
                  pl.BlockSpec((None, d, tn), lambda i, j: (layer, 0, j)),
                  pl.BlockSpec((None, d, tn), lambda i, j: (layer, 0, nj + j)),
                  pl.BlockSpec((None, d, tn), lambda i, j: (layer, 0, 2 * nj + j)),
                  pl.BlockSpec((None, 3, tn), lambda i, j: (layer, 0, j))],
        out_specs=[pl.BlockSpec((tm, tn), lambda i, j: (i, j)),
                   pl.BlockSpec((None, 2, tn), lambda i, j: (i, 0, j))],
        out_shape=[jax.ShapeDtypeStruct((m, cd), BF16), jax.ShapeDtypeStruct((m // tm, 2, cd), F32)],
        scratch_shapes=[pltpu.VMEM((nj, 2, tn), F32)],
        compiler_params=_cparams("arbitrary", "arbitrary"),
        name="conv_prompt",
    )(xn, w_in_a, w_in_a, w_in_a, conv_w)
    return y, tail[tps - 1::tps]


def _conv_step_body(z_ref, s_ref, cw_ref, y_ref, ns_ref, *, cd):
    b, c, hv = z_ref[:, 0:cd], z_ref[:, cd:2 * cd], z_ref[:, 2 * cd:3 * cd]
    u = c * hv
    s0, s1 = s_ref[:, 0, :], s_ref[:, 1, :]
    cw = cw_ref[...]
    y = cw[0:1] * s0 + cw[1:2] * s1 + cw[2:3] * u
    y_ref[...] = b * y
    ns_ref[:, 0, :] = s1
    ns_ref[:, 1, :] = u


def _conv_step(z, state, cw):
    db, _, cd = state.shape
    return pl.pallas_call(
        functools.partial(_conv_step_body, cd=cd),
        out_shape=[jax.ShapeDtypeStruct((db, cd), F32), jax.ShapeDtypeStruct((db, 2, cd), F32)],
        compiler_params=pltpu.CompilerParams(vmem_limit_bytes=VMEM_LIMIT),
        name="conv_step",
    )(z, state, cw)


def _mem_attend_body(q_ref, g_ref, k_ref, v_ref, o_ref, *, heads):
    hd = q_ref.shape[-1] // heads
    scale = hd ** -0.5
    for h in range(heads):
        sl = slice(h * hd, (h + 1) * hd)
        q = _rms(q_ref[:, sl].astype(F32), g_ref[...]).astype(BF16)
        s = _dot_nt(q, k_ref[:, sl].astype(BF16)) * scale
        s = s - jnp.max(s, axis=-1, keepdims=True)
        p = jnp.exp(s)
        p = p / jnp.sum(p, axis=-1, keepdims=True)
        o_ref[:, sl] = _dot(p.astype(BF16), v_ref[:, sl].astype(BF16)).astype(o_ref.dtype)


def _mem_attend(q3, gain, k3, v3, heads, out_dtype, tq=512):
    b, t, w = q3.shape
    ml = k3.shape[1]
    tq = _tile(t, tq)
    hd = w // heads
    return pl.pallas_call(
        functools.partial(_mem_attend_body, heads=heads),
        grid=(b, t // tq),
        in_specs=[pl.BlockSpec((None, tq, w), lambda i, j: (i, j, 0)),
                  pl.BlockSpec((1, hd), lambda i, j: (0, 0)),
                  pl.BlockSpec((None, ml, w), lambda i, j: (i, 0, 0)),
                  pl.BlockSpec((None, ml, w), lambda i, j: (i, 0, 0))],
        out_specs=pl.BlockSpec((None, tq, w), lambda i, j: (i, j, 0)),
        out_shape=jax.ShapeDtypeStruct((b, t, w), out_dtype),
        compiler_params=_cparams("arbitrary", "arbitrary"),
        name="mem_attend",
    )(q3, gain.reshape(1, hd), k3, v3)


def _log_sigmoid(x):
    return -(jnp.maximum(-x, 0.0) + jnp.log1p(jnp.exp(-jnp.abs(x))))


def _logf_body(x_ref, w_ref, b_ref, o_ref, opad_ref):
    z = _dot(x_ref[...].astype(BF16), w_ref[...].astype(BF16))
    lf = _log_sigmoid(z + b_ref[...])
    opad_ref[...] = lf
    o_ref[...] = lf[:, :o_ref.shape[1]]


def _logf(x, w_pad, b_pad, h, tm=1024):
    m, k = x.shape
    tm = _tile(m, tm)
    return pl.pallas_call(
        _logf_body,
        grid=(m // tm,),
        in_specs=[pl.BlockSpec((tm, k), lambda i: (i, 0)),
                  pl.BlockSpec((k, LANE), lambda i: (0, 0)),
                  pl.BlockSpec((1, LANE), lambda i: (0, 0))],
        out_specs=[pl.BlockSpec((tm, h), lambda i: (i, 0)), pl.BlockSpec((tm, LANE), lambda i: (i, 0))],
        out_shape=[jax.ShapeDtypeStruct((m, h), F32), jax.ShapeDtypeStruct((m, LANE), F32)],
        compiler_params=_cparams("arbitrary"),
        name="logf",
    )(x, w_pad, b_pad)


def _cumsum_body(x_ref, o_ref):
    x = x_ref[...]
    t = x.shape[0]
    row = lax.broadcasted_iota(jnp.int32, x.shape, 0)
    s = 1
    while s < t:
        x = x + jnp.where(row >= s, pltpu.roll(x, s, 0), 0.0)
        s *= 2
    o_ref[...] = x


def _cumsum_time(x3):
    b, t, w = x3.shape
    return pl.pallas_call(
        _cumsum_body,
        grid=(b,),
        in_specs=[pl.BlockSpec((None, t, w), lambda i: (i, 0, 0))],
        out_specs=pl.BlockSpec((None, t, w), lambda i: (i, 0, 0)),
        out_shape=jax.ShapeDtypeStruct((b, t, w), F32),
        compiler_params=_cparams("arbitrary"),
        name="cumsum_time",
    )(x3)


def _fox_prompt_body(q_ref, k_ref, vt_ref, cq_ref, ck_ref, o_ref, ckb_ref, *, group, hd, tk):
    qi = pl.program_id(2)
    tq = q_ref.shape[0]
    cols = group * tq
    scale = hd ** -0.5
    q3 = jnp.concatenate([q_ref[:, g * hd:(g + 1) * hd] for g in range(group)], axis=0)
    cq = jnp.concatenate([cq_ref[g:g + 1, :] for g in range(group)], axis=1)

    @pl.when(qi == 0)
    def _():
        for g in range(group):
            ckb_ref[g] = jnp.broadcast_to(ck_ref[:, g:g + 1], ckb_ref.shape[1:])

    qpos = qi * tq + jnp.concatenate([lax.broadcasted_iota(jnp.int32, (tk, tq), 1)] * group, axis=1)
    krow = lax.broadcasted_iota(jnp.int32, (tk, cols), 0)

    def qk(ki):
        off = pl.multiple_of(ki * tk, tk)
        return _dot_nt(k_ref[pl.ds(off, tk), :].astype(BF16), q3)

    def softmax(ki, qk_raw, m, l, masked):
        off = pl.multiple_of(ki * tk, tk)
        ckb = jnp.concatenate([ckb_ref[g, pl.ds(off, tk), :] for g in range(group)], axis=1)
        s = qk_raw * scale + (cq - ckb)
        if masked:
            s = jnp.where(off + krow <= qpos, s, NEG_INF)
        m_new = jnp.maximum(m, jnp.max(s, axis=0, keepdims=True))
        alpha = jnp.exp(m - m_new)
        p = jnp.exp(s - m_new)
        l = alpha * l + jnp.sum(p, axis=0, keepdims=True)
        return m_new, l, alpha, p.astype(BF16)

    def pv(ki, p, alpha, acc):
        off = pl.multiple_of(ki * tk, tk)
        return alpha * acc + _dot(vt_ref[:, pl.ds(off, tk)].astype(BF16), p)

    def body(ki, carry):
        m, l, acc, qk_raw, p_prev, alpha_prev = carry
        qk_next = qk(ki + 1)
        acc = pv(jnp.maximum(ki - 1, 0), p_prev, alpha_prev, acc)
        m, l, alpha, p = softmax(ki, qk_raw, m, l, False)
        return m, l, acc, qk_next, p, alpha

    init = (jnp.full((1, cols), NEG_INF, F32), jnp.zeros((1, cols), F32), jnp.zeros((hd, cols), F32),
            qk(0), jnp.zeros((tk, cols), BF16), jnp.ones((1, cols), F32))
    n_full = (qi * tq) // tk
    m, l, acc, qk_raw, p_prev, alpha_prev = lax.fori_loop(0, n_full, body, init)
    acc = pv(jnp.maximum(n_full - 1, 0), p_prev, alpha_prev, acc)
    m, l, alpha, p = softmax(n_full, qk_raw, m, l, True)
    acc = pv(n_full, p, alpha, acc)
    out_t = acc / l
    for g in range(group):
        o_ref[:, g * hd:(g + 1) * hd] = out_t[:, g * tq:(g + 1) * tq].T.astype(o_ref.dtype)


def _fox_prompt(qn, k2, vt, c_col, c_row, batch, seq, kvh, group, hd, tk=256):
    m = qn.shape[0]
    tq = _tile(seq, LANE)
    tk = _tile(seq, tk)
    assert tk % tq == 0
    nq = seq // tq
    gw = group * hd
    return pl.pallas_call(
        functools.partial(_fox_prompt_body, group=group, hd=hd, tk=tk),
        grid=(batch, kvh, nq),
        in_specs=[pl.BlockSpec((tq, gw), lambda b, h, i: (b * nq + i, h)),
                  pl.BlockSpec((seq, hd), lambda b, h, i: (b, h)),
                  pl.BlockSpec((None, None, hd, seq), lambda b, h, i: (b, h, 0, 0)),
                  pl.BlockSpec((None, None, group, tq), lambda b, h, i: (b, h, 0, i)),
                  pl.BlockSpec((None, None, seq, group), lambda b, h, i: (b, h, 0, 0))],
        out_specs=pl.BlockSpec((tq, gw), lambda b, h, i: (b * nq + i, h)),
        out_shape=jax.ShapeDtypeStruct((m, kvh * gw), BF16),
        scratch_shapes=[pltpu.VMEM((group, seq, tq), F32)],
        compiler_params=_cparams("arbitrary", "arbitrary", "arbitrary"),
        name="fox_prompt",
    )(qn, k2, vt, c_row, c_col)


def _fox_decode_body(pt_ref, q_ref, kn_ref, vn_ref, lfn_ref, *refs, kvh, group, hd, pages):
    k_refs, v_refs, lf_refs = refs[:pages], refs[pages:2 * pages], refs[2 * pages:3 * pages]
    o_ref, m_ref, l_ref, acc_ref, d_ref = refs[3 * pages:]
    p = pl.program_id(1)
    heads = kvh * group
    ps = lf_refs[0].shape[1]
    scale = hd ** -0.5
    head_kv = lax.broadcasted_iota(jnp.int32, (heads, hd), 0) // group
    q = q_ref[...]

    def per_head(row_ref):
        out = jnp.zeros((heads, hd), F32)
        for j in range(kvh):
            out = out + jnp.where(head_kv == j, row_ref[:, j * hd:(j + 1) * hd], 0.0)
        return out

    @pl.when(p == 0)
    def _():
        m_ref[...] = jnp.sum(q.astype(F32) * per_head(kn_ref), axis=1, keepdims=True) * scale
        l_ref[...] = jnp.ones_like(l_ref)
        acc_ref[...] = per_head(vn_ref)
        d_ref[...] = lfn_ref[...]

    col_kv = lax.broadcasted_iota(jnp.int32, (heads, kvh * ps), 1) // ps
    own = col_kv == lax.broadcasted_iota(jnp.int32, (heads, kvh * ps), 0) // group
    lane = lax.broadcasted_iota(jnp.int32, (heads, ps), 1)
    qb = q.astype(BF16)
    d = d_ref[...]
    scores, values = [], []
    for k_ref, v_ref, lf_ref in zip(k_refs, v_refs, lf_refs):
        lf = lf_ref[...]
        suf = lf
        s = 1
        while s < ps:
            suf = suf + jnp.where(lane < ps - s, pltpu.roll(suf, ps - s, 1), 0.0)
            s *= 2
        bias = d + (suf - lf)
        d = d + suf[:, 0:1]
        kcat = jnp.concatenate([k_ref[pl.ds(j, ps, stride=kvh), :] for j in range(kvh)], axis=0).astype(BF16)
        values.append(jnp.concatenate([v_ref[pl.ds(j, ps, stride=kvh), :] for j in range(kvh)],
                                      axis=0).astype(BF16))
        sc = _dot_nt(qb, kcat) * scale + jnp.concatenate([bias] * kvh, axis=1)
        scores.append(jnp.where(own, sc, NEG_INF))
    d_ref[...] = d
    m_old = m_ref[...]
    m_new = m_old
    for sc in scores:
        m_new = jnp.maximum(m_new, jnp.max(sc, axis=-1, keepdims=True))
    alpha = jnp.exp(m_old - m_new)
    l_new = alpha * l_ref[...]
    acc = alpha * acc_ref[...]
    for sc, vcat in zip(scores, values):
        pr = jnp.exp(sc - m_new)
        l_new = l_new + jnp.sum(pr, axis=-1, keepdims=True)
        acc = acc + _dot(pr.astype(BF16), vcat)
    l_ref[...] = l_new
    acc_ref[...] = acc
    m_ref[...] = m_new

    @pl.when(p == pl.num_programs(1) - 1)
    def _():
        o_ref[...] = acc_ref[...] / l_ref[...]


def _fox_decode(page_table, q3, k_new, v_new, lf_new, cache_k, cache_v, cache_lf_t, kvh, group, hd, pages=8):
    db, heads, _ = q3.shape
    npg = page_table.shape[1]
    ps = cache_lf_t.shape[2]
    w = kvh * hd
    while npg % pages:
        pages -= 1

    def page3(i):
        return lambda b, p, pt: (pt[b, npg - 1 - (p * pages + i)], 0, 0)

    grid_spec = pltpu.PrefetchScalarGridSpec(
        num_scalar_prefetch=1,
        grid=(db, npg // pages),
        in_specs=[pl.BlockSpec((None, heads, hd), lambda b, p, pt: (b, 0, 0)),
                  pl.BlockSpec((None, 1, w), lambda b, p, pt: (b, 0, 0)),
                  pl.BlockSpec((None, 1, w), lambda b, p, pt: (b, 0, 0)),
                  pl.BlockSpec((None, heads, 1), lambda b, p, pt: (b, 0, 0))]
                 + [pl.BlockSpec((None, ps * kvh, hd), page3(i)) for i in range(pages)]
                 + [pl.BlockSpec((None, ps * kvh, hd), page3(i)) for i in range(pages)]
                 + [pl.BlockSpec((None, heads, ps), page3(i)) for i in range(pages)],
        out_specs=pl.BlockSpec((None, heads, hd), lambda b, p, pt: (b, 0, 0)),
        scratch_shapes=[pltpu.VMEM((heads, 1), F32), pltpu.VMEM((heads, 1), F32),
                        pltpu.VMEM((heads, hd), F32), pltpu.VMEM((heads, 1), F32)],
    )
    return pl.pallas_call(
        functools.partial(_fox_decode_body, kvh=kvh, group=group, hd=hd, pages=pages),
        grid_spec=grid_spec,
        out_shape=jax.ShapeDtypeStruct((db, heads, hd), F32),
        compiler_params=_cparams("arbitrary", "arbitrary"),
        name="fox_decode",
    )(page_table, q3, k_new, v_new, lf_new, *([cache_k] * pages), *([cache_v] * pages), *([cache_lf_t] * pages))


def _router_body(x_ref, xs_ref, g_ref, w_ref, b_ref, hn_ref, idx_ref, gate_ref, *, n_exp):
    tm, ms = x_ref.shape[0], xs_ref.shape[0]
    tail = jnp.concatenate([xs_ref[...], jnp.zeros((tm - ms, xs_ref.shape[1]), F32)], axis=0)
    x = jnp.where(pl.program_id(0) == pl.num_programs(0) - 1, tail, x_ref[...])
    y = _rms(x, g_ref[...])
    hn_ref[...] = y
    logits = jnp.dot(y, w_ref[...], preferred_element_type=F32,
                     precision=lax.Precision.HIGHEST) + b_ref[...]
    lane = lax.broadcasted_iota(jnp.int32, logits.shape, 1)
    logits = jnp.where(lane < n_exp, logits, NEG_INF)
    m1 = jnp.max(logits, axis=-1, keepdims=True)
    i1 = jnp.min(jnp.where(logits == m1, lane, LANE), axis=-1, keepdims=True)
    rest = jnp.where(lane == i1, NEG_INF, logits)
    m2 = jnp.max(rest, axis=-1, keepdims=True)
    i2 = jnp.min(jnp.where(rest == m2, lane, LANE), axis=-1, keepdims=True)
    e = jnp.exp(m2 - m1)
    g1 = 1.0 / (1.0 + e)
    g2 = e / (1.0 + e)
    idx_ref[...] = jnp.where(lane == 0, i1, jnp.where(lane == 1, i2, 0))
    gate_ref[...] = jnp.where(lane == 0, g1, jnp.where(lane == 1, g2, 0.0))


def _router(x, xs, gain, w_pad, b_pad, n_exp):
    m, k = x.shape
    tm = _tile(m, 256)
    nt = m // tm
    assert xs.shape[0] <= tm
    rows = m + tm
    return pl.pallas_call(
        functools.partial(_router_body, n_exp=n_exp),
        grid=(nt + 1,),
        in_specs=[pl.BlockSpec((tm, k), lambda i: (jnp.minimum(i, nt - 1), 0)),
                  pl.BlockSpec(xs.shape, lambda i: (0, 0)),
                  pl.BlockSpec((1, k), lambda i: (0, 0)),
                  pl.BlockSpec((k, LANE), lambda i: (0, 0)),
                  pl.BlockSpec((1, LANE), lambda i: (0, 0))],
        out_specs=[pl.BlockSpec((tm, k), lambda i: (i, 0)),
                   pl.BlockSpec((tm, LANE), lambda i: (i, 0)), pl.BlockSpec((tm, LANE), lambda i: (i, 0))],
        out_shape=[jax.ShapeDtypeStruct((rows, k), F32),
                   jax.ShapeDtypeStruct((rows, LANE), jnp.int32), jax.ShapeDtypeStruct((rows, LANE), F32)],
        compiler_params=_cparams("arbitrary"),
        name="router",
    )(x, xs, gain.reshape(1, k), w_pad, b_pad)


def _gather_body(tok_ref, used_ref, a_ref, o_ref, buf_ref, sem, *, tg):
    i = pl.program_id(0)
    slot = i % 2

    def row_copy(t, r, s):
        return pltpu.make_async_copy(a_ref.at[pl.ds(t, 1)], buf_ref.at[s, pl.ds(r, 1)], sem.at[s])

    def issue_tile(tile, s):
        base = tile * tg

        @pl.when(base < used_ref[0])
        def _():
            def issue(r8, c):
                for u in range(8):
                    r = r8 * 8 + u
                    row_copy(tok_ref[base + r], r, s).start(priority=u % 2)
                return c

            lax.fori_loop(0, tg // 8, issue, 0)

    @pl.when(i == 0)
    def _():
        issue_tile(0, 0)

    @pl.when(i + 1 < pl.num_programs(0))
    def _():
        issue_tile(i + 1, 1 - slot)

    @pl.when(i * tg < used_ref[0])
    def _():
        def drain(r, c):
            row_copy(0, r, slot).wait()
            return c

        lax.fori_loop(0, tg, drain, 0, unroll=8)
        o_ref[...] = buf_ref[slot].astype(o_ref.dtype)

    @pl.when(i * tg >= used_ref[0])
    def _():
        o_ref[...] = jnp.zeros_like(o_ref)


def _gather_rows(row_token, used_rows, src, tg=256):
    p = row_token.shape[0]
    d = src.shape[1]
    tg = _tile(p, tg)
    grid_spec = pltpu.PrefetchScalarGridSpec(
        num_scalar_prefetch=2,
        grid=(p // tg,),
        in_specs=[pl.BlockSpec(memory_space=pl.ANY)],
        out_specs=pl.BlockSpec((tg, d), lambda i, tok, used: (i, 0)),
        scratch_shapes=[pltpu.VMEM((2, tg, d), F32), pltpu.SemaphoreType.DMA((2,))],
    )
    return pl.pallas_call(
        functools.partial(_gather_body, tg=tg),
        grid_spec=grid_spec,
        out_shape=jax.ShapeDtypeStruct((p, d), BF16),
        compiler_params=_cparams("arbitrary"),
        name="moe_gather",
    )(row_token, used_rows, src)


def _combine_body(pos_ref, y_ref, h_ref, g_ref, o_ref, buf_ref, sem, *, tt):
    i = pl.program_id(0)
    base = i * tt

    def row_copy(k, src_row, r):
        return pltpu.make_async_copy(y_ref.at[pl.ds(src_row, 1)], buf_ref.at[k, pl.ds(r, 1)], sem)

    def issue(r, c):
        row_copy(0, pos_ref[2 * (base + r)], r).start(priority=0)
        row_copy(1, pos_ref[2 * (base + r) + 1], r).start(priority=1)
        return c

    lax.fori_loop(0, tt, issue, 0)

    def drain(r, c):
        row_copy(0, 0, r).wait()
        row_copy(1, 0, r).wait()
        return c

    lax.fori_loop(0, tt, drain, 0)
    g = g_ref[...]
    o_ref[...] = h_ref[...] + (g[:, 0:1] * buf_ref[0] + g[:, 1:2] * buf_ref[1])


def _combine(pos_flat, y_sorted, h, gates, tt=256):
    m, d = h.shape
    tt = _tile(m, tt)
    grid_spec = pltpu.PrefetchScalarGridSpec(
        num_scalar_prefetch=1,
        grid=(m // tt,),
        in_specs=[pl.BlockSpec(memory_space=pl.ANY),
                  pl.BlockSpec((tt, d), lambda i, pos: (i, 0)),
                  pl.BlockSpec((tt, LANE), lambda i, pos: (i, 0))],
        out_specs=pl.BlockSpec((tt, d), lambda i, pos: (i, 0)),
        scratch_shapes=[pltpu.VMEM((2, tt, d), F32), pltpu.SemaphoreType.DMA(())],
    )
    return pl.pallas_call(
        functools.partial(_combine_body, tt=tt),
        grid_spec=grid_spec,
        out_shape=jax.ShapeDtypeStruct((m, d), F32),
        compiler_params=_cparams("arbitrary"),
        name="moe_combine",
    )(pos_flat, y_sorted, h, gates)


def _for_row_blocks(nrows, total, sizes, live, dead=None):
    def run(fn, size, lo, hi):
        def body(t, carry):
            fn(pl.ds(pl.multiple_of(t * size, size), size), size)
            return carry
        lax.fori_loop(lo, hi, body, 0)

    small = sizes[-1]
    end = (nrows + small - 1) // small
    done = 0
    for size in sizes:
        per = size // small
        n = (end - done) // per
        run(live, size, done // per, done // per + n)
        done = done + n * per
    if dead is not None:
        run(dead, small, end, total // small)


def _moe_glu_body(ce_ref, cx_ref, cv_ref, used_ref, x_ref, wg_ref, wu_ref, o_ref, *, sizes):
    c = pl.program_id(0)

    @pl.when(c < used_ref[0])
    def _():
        def live(rows, size):
            x = x_ref[rows, :]
            g = _dot(x, wg_ref[...].astype(BF16))
            u = _dot(x, wu_ref[...].astype(BF16))
            o_ref[rows, :] = _silu_mul(g, u).astype(o_ref.dtype)

        def dead(rows, size):
            o_ref[rows, :] = jnp.zeros((size, o_ref.shape[1]), o_ref.dtype)

        _for_row_blocks(cv_ref[c], x_ref.shape[0], sizes, live, dead)

    @pl.when(c >= used_ref[0])
    def _():
        o_ref[...] = jnp.zeros_like(o_ref)


def _moe_glu(sched, x_sorted, we_gu, chunk, sizes, tn=256):
    ce, cx, cv, used = sched
    p, d = x_sorted.shape
    f = we_gu.shape[2] // 2
    tn = _tile(f, tn)
    nj = f // tn
    nc = p // chunk

    def jpin(c, j, used):
        return jnp.where(c < used[0], j, nj - 1)

    grid_spec = pltpu.PrefetchScalarGridSpec(
        num_scalar_prefetch=4,
        grid=(nc, nj),
        in_specs=[pl.BlockSpec((chunk, d), lambda c, j, ce, cx, cv, used: (cx[c], 0),
                               pipeline_mode=pl.Buffered(1)),
                  pl.BlockSpec((None, d, tn), lambda c, j, ce, cx, cv, used: (ce[c], 0, jpin(c, j, used))),
                  pl.BlockSpec((None, d, tn), lambda c, j, ce, cx, cv, used: (ce[c], 0, nj + jpin(c, j, used)))],
        out_specs=pl.BlockSpec((chunk, tn), lambda c, j, ce, cx, cv, used: (c, j)),
    )
    return pl.pallas_call(
        functools.partial(_moe_glu_body, sizes=sizes),
        grid_spec=grid_spec,
        out_shape=jax.ShapeDtypeStruct((p, f), BF16),
        compiler_params=_cparams("arbitrary", "arbitrary"),
        name="moe_glu",
    )(ce, cx, cv, used, x_sorted, we_gu, we_gu)


def _moe_down_body(ce_ref, cx_ref, cv_ref, used_ref, x_ref, w_ref, o_ref, *, sizes):
    c, kk = pl.program_id(0), pl.program_id(2)

    @pl.when(c < used_ref[0])
    def _():
        def first(rows, size):
            o_ref[rows, :] = _dot(x_ref[rows, :], w_ref[...].astype(BF16))

        def accumulate(rows, size):
            o_ref[rows, :] += _dot(x_ref[rows, :], w_ref[...].astype(BF16))

        def dead(rows, size):
            o_ref[rows, :] = jnp.zeros((size, o_ref.shape[1]), o_ref.dtype)

        @pl.when(kk == 0)
        def _():
            _for_row_blocks(cv_ref[c], x_ref.shape[0], sizes, first, dead)

        @pl.when(kk > 0)
        def _():
            _for_row_blocks(cv_ref[c], x_ref.shape[0], sizes, accumulate)

    @pl.when((c >= used_ref[0]) & (kk == 0))
    def _():
        o_ref[...] = jnp.zeros_like(o_ref)


def _moe_down(sched, act_sorted, we_down, chunk, sizes, tn=512, tk=2048):
    ce, cx, cv, used = sched
    p, f = act_sorted.shape
    d = we_down.shape[2]
    tn, tk = _tile(d, tn), _tile(f, tk)
    nj, nk = d // tn, f // tk
    nc = p // chunk

    def pin(c, v, last, used):
        return jnp.where(c < used[0], v, last)

    grid_spec = pltpu.PrefetchScalarGridSpec(
        num_scalar_prefetch=4,
        grid=(nc, nj, nk),
        in_specs=[pl.BlockSpec((chunk, tk), lambda c, j, k, ce, cx, cv, used: (cx[c], pin(c, k, nk - 1, used))),
                  pl.BlockSpec((None, tk, tn), lambda c, j, k, ce, cx, cv, used:
                               (ce[c], pin(c, k, nk - 1, used), pin(c, j, nj - 1, used)))],
        out_specs=pl.BlockSpec((chunk, tn), lambda c, j, k, ce, cx, cv, used: (c, j)),
    )
    return pl.pallas_call(
        functools.partial(_moe_down_body, sizes=sizes),
        grid_spec=grid_spec,
        out_shape=jax.ShapeDtypeStruct((p, d), F32),
        compiler_params=_cparams("arbitrary", "arbitrary", "arbitrary"),
        name="moe_down",
    )(ce, cx, cv, used, act_sorted, we_down)


def _moe(h_p, h_s, gain, w_router, b_router, we_gu, we_down):
    mp, d = h_p.shape
    ms = h_s.shape[0]
    n_exp = we_gu.shape[0]
    mt = mp + ms
    w_pad = jnp.pad(w_router, ((0, 0), (0, LANE - n_exp)))
    b_pad = jnp.pad(b_router.reshape(1, n_exp).astype(F32), ((0, 0), (0, LANE - n_exp)))
    hn, idx_all, gate_all = _router(h_p, h_s, gain, w_pad, b_pad, n_exp)

    small = MOE_BLOCKS[-1]
    chunk = min(MOE_CHUNK, -(-2 * mt // small) * small)
    sizes = tuple(s for s in MOE_BLOCKS if s <= chunk)
    idx = idx_all[:mt, :2]
    sel = (idx[:, :, None] == jnp.arange(n_exp, dtype=jnp.int32)[None, None, :]).any(axis=1).astype(jnp.int32)
    rank = jnp.cumsum(sel, axis=0) - sel
    cnt = jnp.sum(sel, axis=0)
    nch = (cnt + chunk - 1) // chunk
    ch_end = jnp.cumsum(nch)
    ch_off = ch_end - nch
    n_chunks = (2 * mt + n_exp * (chunk - 1)) // chunk
    p_rows = n_chunks * chunk
    pos = jnp.take_along_axis(ch_off[None, :] * chunk + rank, idx, axis=1)
    tok = jnp.repeat(jnp.arange(mt, dtype=jnp.int32), 2)
    row_token = jnp.zeros((p_rows,), jnp.int32).at[pos.reshape(-1)].set(tok)
    used = ch_end[-1].astype(jnp.int32)
    cid = jnp.minimum(jnp.arange(n_chunks, dtype=jnp.int32), used - 1)
    ce = jnp.minimum(jnp.sum((ch_end[None, :] <= cid[:, None]).astype(jnp.int32), axis=1), n_exp - 1)
    cv = jnp.clip(cnt[ce] - (cid - ch_off[ce]) * chunk, 0, chunk).astype(jnp.int32)
    sched = (ce, cid, cv, used.reshape(1))

    x_sorted = _gather_rows(row_token, (used * chunk).reshape(1), hn)
    act = _moe_glu(sched, x_sorted, we_gu, chunk, sizes)
    y_sorted = _moe_down(sched, act, we_down, chunk, sizes)
    pos = pos.astype(jnp.int32)
    out_p = _combine(pos[:mp].reshape(-1), y_sorted, h_p, gate_all[:mp])
    out_s = _combine(pos[mp:].reshape(-1), y_sorted, h_s, gate_all[mp:mt])
    return out_p, out_s


def kernel(x_prompt, x_sample, mem_prompt, cache_k, cache_v, cache_logf, state_conv, cache_mem_k, cache_mem_v, page_table, norm_mix, norm_ffn, w_in_a, conv_w, w_in_b, q_norm_b, w_out, w_mem_kv, mem_q_norm, mem_k_norm, kv_norm, w_kv, b_f, k_norm, w_gu, w_down, w_router, b_router, we_gu, we_down):
    bsz, seq, d = x_prompt.shape
    db, dec_seq, _ = x_sample.shape
    assert dec_seq == 1, "the decode path handles one new token per sequence"
    depth, _, ml, mem_heads, mem_hd = cache_mem_k.shape
    assert depth == 2 and w_in_a.shape[0] == 1 and w_in_b.shape[0] == 1
    mem_w = mem_heads * mem_hd
    n_pool, ps, kvh, hd = cache_k.shape
    heads = cache_logf.shape[2]
    group = heads // kvh
    kv_w = kvh * hd
    cd = conv_w.shape[2]
    assert conv_w.shape[1] == 3 and state_conv.shape[2] == 2
    mp = bsz * seq

    mem_bf = mem_prompt.reshape(bsz * ml, d).astype(BF16)
    mk, mv = [], []
    for i in range(depth):
        mk.append(_mm_headnorm(mem_bf, w_mem_kv, i, 0, mem_w, mem_k_norm[i], mem_hd, F32))
        mv.append(_mm(mem_bf, w_mem_kv, i, mem_w, mem_w, F32))
    mem_k_prompt = jnp.stack(mk).reshape(depth, bsz, ml, mem_heads, mem_hd)
    mem_v_prompt = jnp.stack(mv).reshape(depth, bsz, ml, mem_heads, mem_hd)

    w_kv3 = w_kv.reshape(1, d, -1)
    w_lf = jnp.pad(w_kv[:, 2 * kv_w:], ((0, 0), (0, LANE - heads)))
    b_lf = jnp.pad(b_f.reshape(1, heads).astype(F32), ((0, 0), (0, LANE - heads)))
    we_gu3 = we_gu.reshape(we_gu.shape[1:])
    we_down3 = we_down.reshape(we_down.shape[1:])

    def layer0_tail(h0, main, q_mem, mem_k0, mem_v0, nb, act_dtype):
        t = h0.shape[0] // nb
        mo = _mem_attend(q_mem.reshape(nb, t, mem_w), mem_q_norm[0], mem_k0.reshape(nb, ml, mem_w),
                         mem_v0.reshape(nb, ml, mem_w), mem_heads, act_dtype).reshape(nb * t, mem_w)
        h1 = _mm_out(main, mo, w_out, 0, h0)
        (hn,) = _rmsnorm(h1, norm_ffn[0], (act_dtype,))
        act = _mm_glu(hn, w_gu, 0, act_dtype)
        h2 = _mm_down(act, w_down, 0, h1)
        (kvn,) = _rmsnorm(h2, kv_norm, (act_dtype,))
        k = _mm_headnorm(kvn, w_kv3, 0, 0, kv_w, k_norm, hd, F32)
        v = _mm(kvn, w_kv3, 0, kv_w, kv_w, F32)
        logf, logf_pad = _logf(kvn, w_lf, b_lf, heads)
        (xn1,) = _rmsnorm(h2, norm_mix[1], (act_dtype,))
        qn = _mm_headnorm(xn1, w_in_b, 0, 0, heads * hd, q_norm_b[0], hd, act_dtype)
        q_mem1 = _mm(xn1, w_in_b, 0, heads * hd, mem_w, F32)
        return h2, k, v, logf, logf_pad, qn, q_mem1

    def layer1_attn_out(h2, attn, q_mem1, mem_k1, mem_v1, nb, act_dtype):
        t = h2.shape[0] // nb
        mo = _mem_attend(q_mem1.reshape(nb, t, mem_w), mem_q_norm[1], mem_k1.reshape(nb, ml, mem_w),
                         mem_v1.reshape(nb, ml, mem_w), mem_heads, act_dtype).reshape(nb * t, mem_w)
        return _mm_out(attn, mo, w_out, 1, h2)

    xp = x_prompt.reshape(mp, d)
    (xn,) = _rmsnorm(xp, norm_mix[0], (BF16,))
    main_p, conv_tail = _conv_prompt(xn, w_in_a, conv_w, 0, bsz, seq)
    q_mem_p = _mm(xn, w_in_a, 0, 3 * cd, mem_w, F32)
    h2_p, k_p, v_p, logf_p, logf_pad_p, qn_p, q_mem1_p = layer0_tail(
        xp, main_p, q_mem_p, mk[0], mv[0], bsz, BF16)
    c = _cumsum_time(logf_pad_p.reshape(bsz, seq, LANE))[:, :, :heads].reshape(bsz, seq, kvh, group)
    vt_p = v_p.reshape(bsz, seq, kvh, hd).transpose(0, 2, 3, 1)
    attn_p = _fox_prompt(qn_p, k_p, vt_p, c.transpose(0, 2, 1, 3), c.transpose(0, 2, 3, 1),
                         bsz, seq, kvh, group, hd)
    h3_p = layer1_attn_out(h2_p, attn_p, q_mem1_p, mk[1], mv[1], bsz, BF16)

    xs = x_sample.reshape(db, d)
    (xn_s,) = _rmsnorm(xs, norm_mix[0], (F32,))
    z_s = _mm(xn_s, w_in_a, 0, 0, 3 * cd + mem_w, F32)
    main_s, conv_new_s = _conv_step(z_s, state_conv[0], conv_w[0])
    h2_s, k_s, v_s, logf_s, _, qn_s, q_mem1_s = layer0_tail(
        xs, main_s, z_s[:, 3 * cd:], cache_mem_k[0], cache_mem_v[0], db, F32)
    attn_s = _fox_decode(page_table, qn_s.reshape(db, heads, hd), k_s.reshape(db, 1, kv_w),
                         v_s.reshape(db, 1, kv_w), logf_s.reshape(db, heads, 1),
                         cache_k.reshape(n_pool, ps * kvh, hd), cache_v.reshape(n_pool, ps * kvh, hd),
                         cache_logf.transpose(0, 2, 1), kvh, group, hd).reshape(db, heads * hd)
    h3_s = layer1_attn_out(h2_s, attn_s, q_mem1_s, cache_mem_k[1], cache_mem_v[1], db, F32)

    y_p, y_s = _moe(h3_p, h3_s, norm_ffn[1], w_router[0], b_router[0], we_gu3, we_down3)

    return (y_p.reshape(bsz, seq, d), y_s.reshape(db, 1, d),
            k_p.reshape(bsz, seq, kvh, hd), v_p.reshape(bsz, seq, kvh, hd), logf_p.reshape(bsz, seq, heads),
            conv_tail.reshape(1, bsz, 2, cd), mem_k_prompt, mem_v_prompt,
            k_s.reshape(db, 1, kvh, hd), v_s.reshape(db, 1, kvh, hd), logf_s.reshape(db, 1, heads),
            conv_new_s.reshape(1, db, 2, cd))
```
